```python
import math
import jax, jax.numpy as jnp
from jax import lax
import numpy as np

D_MODEL = 2048
BATCH = 4
SEQ = 4096
DEPTH = 4

N_A_LAYERS = DEPTH // 2
N_B_LAYERS = DEPTH - N_A_LAYERS
RMS_EPS = 1e-6

E_A = D_MODEL
POOL_WINDOWS = (2, 4, 8, 16)
N_POOL_GROUPS = len(POOL_WINDOWS)
POOL_GROUP_DIM = E_A // N_POOL_GROUPS

HEAD_DIM = 128
HEADS_PER_GROUP = D_MODEL // HEAD_DIM
DILATED_PAIRS = ((128, 1), (512, 4), (2048, 16))
N_DIL_GROUPS = len(DILATED_PAIRS)
E_B = HEADS_PER_GROUP * HEAD_DIM
ROPE_THETA = 10000.0
NEG_INF = -1e30

kernel_name = "yoco_pool_dilated_hybrid"


def rmsnorm(x, g):
    xf = x.astype(jnp.float32)
    inv = lax.rsqrt(jnp.mean(xf * xf, axis=-1, keepdims=True) + RMS_EPS)
    return (xf * inv).astype(x.dtype) * g


def rope_tables(seq):
    inv_freq = 1.0 / (ROPE_THETA ** (jnp.arange(0, HEAD_DIM, 2, dtype=jnp.float32) / HEAD_DIM))
    ang = jnp.arange(seq, dtype=jnp.float32)[:, None] * inv_freq[None, :]
    return jnp.cos(ang), jnp.sin(ang)


def apply_rope(t, cos, sin):
    tf = t.astype(jnp.float32)
    t1, t2 = tf[..., : HEAD_DIM // 2], tf[..., HEAD_DIM // 2:]
    c, s = cos[None, :, None, :], sin[None, :, None, :]
    return jnp.concatenate([t1 * c - t2 * s, t2 * c + t1 * s], axis=-1).astype(t.dtype)


def multiscale_causal_pool(u):
    b, s, _ = u.shape
    u4 = u.reshape(b, s, N_POOL_GROUPS, POOL_GROUP_DIM)
    csum = jnp.cumsum(u4.astype(jnp.float32), axis=1)
    csum = jnp.concatenate([jnp.zeros_like(csum[:, :1]), csum], axis=1)
    win = jnp.asarray(POOL_WINDOWS, dtype=jnp.int32)
    t1 = jnp.arange(1, s + 1, dtype=jnp.int32)[:, None]
    lower = jnp.maximum(t1 - win[None, :], 0)
    c_lo = csum[:, lower, jnp.arange(N_POOL_GROUPS)[None, :], :]
    count = jnp.minimum(t1, win[None, :]).astype(jnp.float32)
    mean = (csum[:, 1:] - c_lo) / count[None, :, :, None]
    return (mean - u4.astype(jnp.float32)).astype(u.dtype)


def dilated_window_attention(q, k, v, window, dilation):
    b, s, h, hd = q.shape
    d = dilation
    nb = window // dilation
    m = s // d
    nblk = -(-m // nb)
    m_pad = nblk * nb

    def residues(t):
        return t.reshape(b, m, d, h, hd).transpose(0, 2, 3, 1, 4)

    qr, kr, vr = residues(q), residues(k), residues(v)
    qb = jnp.pad(qr, ((0, 0), (0, 0), (0, 0), (0, m_pad - m), (0, 0))).reshape(b, d, h, nblk, nb, hd)
    kv_pad = ((0, 0), (0, 0), (0, 0), (nb, m_pad - m), (0, 0))
    kp = jnp.pad(kr, kv_pad).reshape(b, d, h, nblk + 1, nb, hd)
    vp = jnp.pad(vr, kv_pad).reshape(b, d, h, nblk + 1, nb, hd)
    kb = jnp.concatenate([kp[:, :, :, :-1], kp[:, :, :, 1:]], axis=4)
    vb = jnp.concatenate([vp[:, :, :, :-1], vp[:, :, :, 1:]], axis=4)

    scores = jnp.einsum('bdhnqc,bdhnkc->bdhnqk', qb, kb).astype(jnp.float32) * (1.0 / math.sqrt(hd))
    r_idx = jnp.arange(nb)[:, None]
    c_idx = jnp.arange(2 * nb)[None, :]
    band = (c_idx >= r_idx) & (c_idx <= r_idx + nb)
    blk = jnp.arange(nblk)[:, None, None]
    mask = band[None] & (blk * nb + c_idx[None] >= nb)
    scores = jnp.where(mask[None, None, None], scores, NEG_INF)
    lse = jax.nn.logsumexp(scores, axis=-1)
    p = jnp.exp(scores - lse[..., None]).astype(v.dtype)
    out = jnp.einsum('bdhnqk,bdhnkc->bdhnqc', p, vb)

    out = out.reshape(b, d, h, m_pad, hd)[:, :, :, :m]
    lse = lse.reshape(b, d, h, m_pad)[:, :, :, :m]
    out = out.transpose(0, 3, 1, 2, 4).reshape(b, s, h, hd)
    lse = lse.transpose(0, 3, 1, 2).reshape(b, s, h)
    return out, lse


def setup_inputs(seed: int = 0) -> dict:
    key = jax.random.key(seed)
    ks = jax.random.split(key, 13)
    f32 = jnp.float32
    x = jax.random.normal(ks[0], (BATCH, SEQ, D_MODEL), f32)
    norm_a = 1.0 + 0.1 * jax.random.normal(ks[1], (N_A_LAYERS, D_MODEL), f32)
    w_in_a = jax.random.normal(ks[2], (N_A_LAYERS, D_MODEL, 2 * E_A), f32) * D_MODEL ** -0.5
    w_grp_a = jax.random.normal(ks[3], (N_A_LAYERS, N_POOL_GROUPS, POOL_GROUP_DIM, POOL_GROUP_DIM), f32) * POOL_GROUP_DIM ** -0.5
    scale_a = 1.0 + 0.1 * jax.random.normal(ks[4], (N_A_LAYERS, E_A), f32)
    w_out_a = jax.random.normal(ks[5], (N_A_LAYERS, E_A, D_MODEL), f32) * E_A ** -0.5
    norm_kv = 1.0 + 0.1 * jax.random.normal(ks[6], (D_MODEL,), f32)
    w_k = jax.random.normal(ks[7], (D_MODEL, E_B), f32) * D_MODEL ** -0.5
    w_v = jax.random.normal(ks[8], (D_MODEL, E_B), f32) * D_MODEL ** -0.5
    norm_b = 1.0 + 0.1 * jax.random.normal(ks[9], (N_B_LAYERS, D_MODEL), f32)
    w_in_b = jax.random.normal(ks[10], (N_B_LAYERS, D_MODEL, N_DIL_GROUPS * E_B + E_B), f32) * D_MODEL ** -0.5
    w_out_b = jax.random.normal(ks[11], (N_B_LAYERS, E_B, D_MODEL), f32) * E_B ** -0.5
    norm_f = 1.0 + 0.1 * jax.random.normal(ks[12], (D_MODEL,), f32)
    return {"x": x, "norm_a": norm_a, "w_in_a": w_in_a, "w_grp_a": w_grp_a, "scale_a": scale_a,
            "w_out_a": w_out_a, "norm_kv": norm_kv, "w_k": w_k, "w_v": w_v, "norm_b": norm_b,
            "w_in_b": w_in_b, "w_out_b": w_out_b, "norm_f": norm_f}


def reference(x, norm_a, w_in_a, w_grp_a, scale_a, w_out_a, norm_kv, w_k, w_v, norm_b, w_in_b, w_out_b, norm_f):
    b, s, _ = x.shape
    cos, sin = rope_tables(s)
    k_shared = None
    v_shared = None
    for layer in range(DEPTH):
        if layer < N_A_LAYERS:
            i = layer
            hdn = rmsnorm(x, norm_a[i])
            proj = hdn @ w_in_a[i]
            u, gate = proj[..., :E_A], proj[..., E_A:]
            pooled = multiscale_causal_pool(u)
            y = jnp.einsum('bsgc,gcd->bsgd', pooled, w_grp_a[i]).reshape(b, s, E_A) * scale_a[i]
            x = x + (y * jax.nn.silu(gate)) @ w_out_a[i]
            if layer == N_A_LAYERS - 1:
                kv_in = rmsnorm(x, norm_kv)
                k_shared = apply_rope((kv_in @ w_k).reshape(b, s, HEADS_PER_GROUP, HEAD_DIM), cos, sin)
                v_shared = (kv_in @ w_v).reshape(b, s, HEADS_PER_GROUP, HEAD_DIM)
        else:
            i = layer - N_A_LAYERS
            hdn = rmsnorm(x, norm_b[i])
            proj = hdn @ w_in_b[i]
            q_all = proj[..., : N_DIL_GROUPS * E_B].reshape(b, s, N_DIL_GROUPS, HEADS_PER_GROUP, HEAD_DIM)
            gate = proj[..., N_DIL_GROUPS * E_B:]
            outs = []
            lses = []
            for g, (window, dilation) in enumerate(DILATED_PAIRS):
                q = apply_rope(q_all[:, :, g], cos, sin)
                o_g, lse_g = dilated_window_attention(q, k_shared, v_shared, window, dilation)
                outs.append(o_g)
                lses.append(lse_g)
            alpha = jax.nn.softmax(jnp.stack(lses, axis=0), axis=0)
            merged = jnp.sum(alpha[..., None].astype(x.dtype) * jnp.stack(outs, axis=0), axis=0)
            merged = merged.reshape(b, s, E_B)
            x = x + (merged * jax.nn.silu(gate)) @ w_out_b[i]
    return rmsnorm(x, norm_f)
```

```python
import functools
import math

import jax
import jax.numpy as jnp
from jax import lax
from jax.experimental import pallas as pl
from jax.experimental.pallas import tpu as pltpu

F32 = jnp.float32
BF16 = jnp.bfloat16

RMS_EPS = 1e-6
POOL_WINDOWS = (2, 4, 8, 16)
POOL_HALO = 16
HEAD_DIM = 128
ROPE_THETA = 10000.0
NEG_INF = -1e30
N_RES = 16
DILATIONS = (1, 4, 16)
BAND = 128
PERM_ROWS = N_RES * N_RES
LANES = 128
VMEM_LIMIT = 56 * 1024 * 1024


def _cparams(n_axes):
    return pltpu.CompilerParams(
        dimension_semantics=("arbitrary",) * n_axes, vmem_limit_bytes=VMEM_LIMIT)


def _silu(g):
    return g / (1.0 + jnp.exp(-g))


def _rmsnorm_kernel(x_ref, g_ref, o_ref):
    xf = x_ref[...]
    inv = lax.rsqrt(jnp.mean(xf * xf, axis=-1, keepdims=True) + RMS_EPS)
    o_ref[...] = ((xf * inv) * g_ref[...]).astype(o_ref.dtype)


def _rmsnorm_call(x, g, bm=512):
    n, d = x.shape
    return pl.pallas_call(
        _rmsnorm_kernel,
        grid=(n // bm,),
        in_specs=[pl.BlockSpec((bm, d), lambda i: (i, 0)),
                  pl.BlockSpec((1, d), lambda i: (0, 0))],
        out_specs=pl.BlockSpec((bm, d), lambda i: (i, 0)),
        out_shape=jax.ShapeDtypeStruct((n, d), BF16),
        compiler_params=_cparams(1),
        name="rmsnorm_first",
    )(x, g.reshape(1, d))


def _emit_norms(xn, gains_ref, p_ref, out_refs, norm_specs, bm):
    inv = lax.rsqrt(jnp.mean(xn * xn, axis=-1, keepdims=True) + RMS_EPS)
    xh = xn * inv
    for k, (permuted, dtype) in enumerate(norm_specs):
        hd = (xh * gains_ref[k:k + 1, :]).astype(dtype)
        if not permuted:
            out_refs[k][...] = hd
            continue
        steps = PERM_ROWS // N_RES
        for sub in range(bm // PERM_ROWS):
            pb = jnp.dot(p_ref[...], hd[sub * PERM_ROWS:(sub + 1) * PERM_ROWS, :],
                         preferred_element_type=F32).astype(dtype)
            for r in range(N_RES):
                out_refs[k][r, sub * steps:(sub + 1) * steps, :] = pb[r * steps:(r + 1) * steps, :]


def _proj_kernel(lhs_ref, w_ref, cos_ref, sin_ref, p_ref, o_ref, *perm_refs,
                 bm, bn, n_rope_blocks, n_col_blocks):
    j = pl.program_id(0)
    acc = jnp.dot(lhs_ref[...], w_ref[...], preferred_element_type=F32)

    def _rope():
        cos = cos_ref[...]
        sin = sin_ref[...]
        for hh in range(bn // HEAD_DIM):
            t = acc[:, hh * HEAD_DIM:(hh + 1) * HEAD_DIM]
            rot = pltpu.roll(t, HEAD_DIM // 2, 1)
            o_ref[:, hh * HEAD_DIM:(hh + 1) * HEAD_DIM] = (t * cos + rot * sin).astype(o_ref.dtype)

    def _plain():
        o_ref[...] = acc.astype(o_ref.dtype)

    if n_rope_blocks == 0:
        _plain()
    elif n_rope_blocks == n_col_blocks:
        _rope()
    else:
        pl.when(j < n_rope_blocks)(_rope)
        pl.when(j >= n_rope_blocks)(_plain)

    if perm_refs:
        op_ref = perm_refs[0]
        steps = PERM_ROWS // N_RES
        for sub in range(bm // PERM_ROWS):
            blk = o_ref[sub * PERM_ROWS:(sub + 1) * PERM_ROWS, :]
            pb = jnp.dot(p_ref[...], blk, preferred_element_type=F32).astype(op_ref.dtype)
            for r in range(N_RES):
                op_ref[r, sub * steps:(sub + 1) * steps, :] = pb[r * steps:(r + 1) * steps, :]


def _proj_call(lhs, w, cos, sin, perm_mat, *, seq, col_off, n_cols, n_rope_cols,
               perm_out, name, bm=512, bn=1024):
    n, k = lhs.shape
    n_col_blocks = n_cols // bn
    blocks_per_seq = seq // bm
    col_blk0 = col_off // bn
    kern = functools.partial(_proj_kernel, bm=bm, bn=bn,
                             n_rope_blocks=n_rope_cols // bn, n_col_blocks=n_col_blocks)
    out_shape = [jax.ShapeDtypeStruct((n, n_cols), BF16)]
    out_specs = [pl.BlockSpec((bm, bn), lambda j, i: (i, j))]
    if perm_out:
        out_shape.append(jax.ShapeDtypeStruct((n // seq, N_RES, seq // N_RES, n_cols), BF16))
        out_specs.append(pl.BlockSpec(
            (None, N_RES, bm // N_RES, bn),
            lambda j, i: (i // blocks_per_seq, 0, i % blocks_per_seq, j)))
    res = pl.pallas_call(
        kern,
        grid=(n_col_blocks, n // bm),
        in_specs=[pl.BlockSpec((bm, k), lambda j, i: (i, 0)),
                  pl.BlockSpec((k, bn), lambda j, i: (0, j + col_blk0)),
                  pl.BlockSpec((bm, HEAD_DIM), lambda j, i: (i % blocks_per_seq, 0)),
                  pl.BlockSpec((bm, HEAD_DIM), lambda j, i: (i % blocks_per_seq, 0)),
                  pl.BlockSpec((PERM_ROWS, PERM_ROWS), lambda j, i: (0, 0))],
        out_specs=out_specs,
        out_shape=out_shape,
        compiler_params=_cparams(2),
        name=name,
    )(lhs, w, cos, sin, perm_mat)
    return res if perm_out else res[0]


def _amix_kernel(u_ref, g_ref, halo_ref, x_ref, wgrp_ref, scale_ref, wout_ref, gains_ref,
                 p_ref, xo_ref, *rest, bm, blocks_per_seq, norm_specs):
    out_refs = rest[:len(norm_specs)]
    ext_ref, h_ref = rest[len(norm_specs):]
    blk = pl.program_id(0) % blocks_per_seq
    gc = u_ref.shape[1] // len(POOL_WINDOWS)

    halo = halo_ref[...].astype(F32)
    ext_ref[0:POOL_HALO, :] = jnp.where(blk == 0, 0.0, halo)
    ext_ref[POOL_HALO:, :] = u_ref[...].astype(F32)

    pos = blk * bm + lax.broadcasted_iota(jnp.int32, (bm, 1), 0)
    for g, w in enumerate(POOL_WINDOWS):
        cols = slice(g * gc, (g + 1) * gc)
        cur = ext_ref[POOL_HALO:POOL_HALO + bm, cols]
        wsum = cur
        for back in range(1, w):
            wsum = wsum + ext_ref[POOL_HALO - back:POOL_HALO - back + bm, cols]
        inv_cnt = 1.0 / jnp.minimum(pos + 1, w).astype(F32)
        pooled = wsum * inv_cnt - cur
        y = jnp.dot(pooled.astype(BF16), wgrp_ref[g], preferred_element_type=F32)
        y = y * scale_ref[:, cols]
        h_ref[:, cols] = (y * _silu(g_ref[:, cols].astype(F32))).astype(BF16)

    xn = x_ref[...] + jnp.dot(h_ref[...], wout_ref[...], preferred_element_type=F32)
    xo_ref[...] = xn
    _emit_norms(xn, gains_ref, p_ref, out_refs, norm_specs, bm)


def _norm_out(n, d, seq, bm, permuted, dtype):
    blocks_per_seq = seq // bm
    if permuted:
        shape = jax.ShapeDtypeStruct((n // seq, N_RES, seq // N_RES, d), dtype)
        spec = pl.BlockSpec((None, N_RES, bm // N_RES, d),
                            lambda i: (i // blocks_per_seq, 0, i % blocks_per_seq, 0))
    else:
        shape = jax.ShapeDtypeStruct((n, d), dtype)
        spec = pl.BlockSpec((bm, d), lambda i: (i, 0))
    return shape, spec


def _const_spec(shape):
    zeros = (0,) * len(shape)
    return pl.BlockSpec(shape, lambda i: zeros, pipeline_mode=pl.Buffered(1))


def _amix_call(proj, x, w_grp, scale, w_out, gains, perm_mat, *, seq, norm_specs, name, bm=256):
    n, d = x.shape
    e = proj.shape[1] // 2
    blocks_per_seq = seq // bm
    halo_per_blk = bm // POOL_HALO
    norm_shapes, norm_out_specs = zip(*[_norm_out(n, d, seq, bm, p, dt) for p, dt in norm_specs])
    kern = functools.partial(_amix_kernel, bm=bm, blocks_per_seq=blocks_per_seq,
                             norm_specs=tuple(norm_specs))
    return pl.pallas_call(
        kern,
        grid=(n // bm,),
        in_specs=[pl.BlockSpec((bm, e), lambda i: (i, 0)),
                  pl.BlockSpec((bm, e), lambda i: (i, 1)),
                  pl.BlockSpec((POOL_HALO, e), lambda i: (jnp.maximum(i * halo_per_blk - 1, 0), 0)),
                  pl.BlockSpec((bm, d), lambda i: (i, 0)),
                  _const_spec(w_grp.shape),
                  _const_spec((1, e)),
                  _const_spec(w_out.shape),
                  _const_spec(gains.shape),
                  _const_spec(perm_mat.shape)],
        out_specs=[pl.BlockSpec((bm, d), lambda i: (i, 0))] + list(norm_out_specs),
        out_shape=[jax.ShapeDtypeStruct((n, d), F32)] + list(norm_shapes),
        scratch_shapes=[pltpu.VMEM((bm + POOL_HALO, e), F32), pltpu.VMEM((bm, e), BF16)],
        compiler_params=_cparams(1),
        name=name,
    )(proj, proj, proj, x, w_grp, scale.reshape(1, e), w_out, gains, perm_mat)


def _bout_kernel(hg_ref, x_ref, wout_ref, gains_ref, p_ref, *rest, bm, emit_x, norm_specs):
    n_out = len(norm_specs) + (1 if emit_x else 0)
    outs = rest[:n_out]
    (h_ref,) = rest[n_out:]
    steps = PERM_ROWS // N_RES
    for sub in range(bm // PERM_ROWS):
        blk = jnp.concatenate(
            [hg_ref[r, sub * steps:(sub + 1) * steps, :] for r in range(N_RES)], axis=0)
        h_ref[sub * PERM_ROWS:(sub + 1) * PERM_ROWS, :] = jnp.dot(
            p_ref[...], blk, preferred_element_type=F32).astype(BF16)
    xn = x_ref[...] + jnp.dot(h_ref[...], wout_ref[...], preferred_element_type=F32)
    if emit_x:
        outs[0][...] = xn
        outs = outs[1:]
    _emit_norms(xn, gains_ref, p_ref, outs, norm_specs, bm)


def _bout_call(hg, x, w_out, gains, perm_mat, *, seq, emit_x, norm_specs, name, bm=256):
    n, d = x.shape
    e = hg.shape[-1]
    blocks_per_seq = seq // bm
    norm_shapes, norm_out_specs = zip(*[_norm_out(n, d, seq, bm, p, dt) for p, dt in norm_specs])
    out_specs, out_shape = list(norm_out_specs), list(norm_shapes)
    if emit_x:
        out_specs.insert(0, pl.BlockSpec((bm, d), lambda i: (i, 0)))
        out_shape.insert(0, jax.ShapeDtypeStruct((n, d), F32))
    kern = functools.partial(_bout_kernel, bm=bm, emit_x=emit_x, norm_specs=tuple(norm_specs))
    return pl.pallas_call(
        kern,
        grid=(n // bm,),
        in_specs=[pl.BlockSpec((None, N_RES, bm // N_RES, e),
                               lambda i: (i // blocks_per_seq, 0, i % blocks_per_seq, 0)),
                  pl.BlockSpec((bm, d), lambda i: (i, 0)),
                  _const_spec(w_out.shape),
                  _const_spec(gains.shape),
                  _const_spec(perm_mat.shape)],
        out_specs=out_specs,
        out_shape=out_shape,
        scratch_shapes=[pltpu.VMEM((bm, e), BF16)],
        compiler_params=_cparams(1),
        name=name,
    )(hg, x, w_out, gains, perm_mat)


def _band_block(q, k_prev, k_cur, v_prev, v_cur, mask_prev, mask_cur):
    scale = 1.0 / math.sqrt(HEAD_DIM)
    nt = (((1,), (1,)), ((), ()))
    s_cur = lax.dot_general(q, k_cur, nt, preferred_element_type=F32) * scale
    s_cur = jnp.where(mask_cur, s_cur, NEG_INF)
    m = jnp.max(s_cur, axis=-1, keepdims=True)
    if k_prev is not None:
        s_prev = lax.dot_general(q, k_prev, nt, preferred_element_type=F32) * scale
        s_prev = jnp.where(mask_prev, s_prev, NEG_INF)
        m = jnp.maximum(m, jnp.max(s_prev, axis=-1, keepdims=True))
    p_cur = jnp.exp(s_cur - m)
    l = jnp.sum(p_cur, axis=-1, keepdims=True)
    acc = jnp.dot(p_cur.astype(v_cur.dtype), v_cur, preferred_element_type=F32)
    if k_prev is not None:
        p_prev = jnp.exp(s_prev - m)
        l = l + jnp.sum(p_prev, axis=-1, keepdims=True)
        acc = acc + jnp.dot(p_prev.astype(v_prev.dtype), v_prev, preferred_element_type=F32)
    return acc / l, m + jnp.log(l)


def _attn_kernel(q0_ref, kn_ref, vn_ref, q1_ref, q2_ref, gate_ref, kp_ref, vp_ref, out_ref,
                 o0_s, l0_s, o1_s, l1_s, o2_s, l2_s):
    seq = q0_ref.shape[0]
    steps = seq // N_RES
    row = lax.broadcasted_iota(jnp.int32, (BAND, BAND), 0)
    col = lax.broadcasted_iota(jnp.int32, (BAND, BAND), 1)
    mask_cur = col <= row
    mask_prev = col >= row
    d4 = DILATIONS[1]
    chunk = BAND // d4
    row4 = d4 * (row % chunk) + row // chunk
    col4 = d4 * (col % chunk) + col // chunk
    mask4_cur = col4 <= row4
    mask4_prev = col4 >= row4

    def bcast(lse):
        return jnp.broadcast_to(lse, (lse.shape[0], LANES))

    o, lse = _band_block(q0_ref[0:BAND, :], None, kn_ref[0:BAND, :], None, vn_ref[0:BAND, :],
                         None, mask_cur)
    o0_s[0:BAND, :] = o
    l0_s[0:BAND, :] = bcast(lse)

    def g0_body(a, carry):
        r0 = pl.multiple_of(a * BAND, BAND)
        rp = pl.multiple_of((a - 1) * BAND, BAND)
        o, lse = _band_block(q0_ref[pl.ds(r0, BAND), :],
                             kn_ref[pl.ds(rp, BAND), :], kn_ref[pl.ds(r0, BAND), :],
                             vn_ref[pl.ds(rp, BAND), :], vn_ref[pl.ds(r0, BAND), :],
                             mask_prev, mask_cur)
        o0_s[pl.ds(r0, BAND), :] = o
        l0_s[pl.ds(r0, BAND), :] = bcast(lse)
        return carry

    lax.fori_loop(1, seq // BAND, g0_body, 0)

    def g2_body(r, carry):
        for nb in range(steps // BAND):
            cur = slice(nb * BAND, (nb + 1) * BAND)
            if nb == 0:
                o, lse = _band_block(q2_ref[r, cur, :], None, kp_ref[r, cur, :], None,
                                     vp_ref[r, cur, :], None, mask_cur)
            else:
                prev = slice((nb - 1) * BAND, nb * BAND)
                o, lse = _band_block(q2_ref[r, cur, :], kp_ref[r, prev, :], kp_ref[r, cur, :],
                                     vp_ref[r, prev, :], vp_ref[r, cur, :], mask_prev, mask_cur)
            o2_s[r, cur, :] = o
            l2_s[r, cur, :] = bcast(lse)
        return carry

    lax.fori_loop(0, N_RES, g2_body, 0)

    n_sub = N_RES // d4

    def gather4(ref, r4, start):
        return jnp.concatenate(
            [ref[r4 + d4 * k, pl.ds(start, chunk), :] for k in range(n_sub)], axis=0)

    def scatter4(o, lse, r4, start):
        lb = bcast(lse)
        for k in range(n_sub):
            o1_s[r4 + d4 * k, pl.ds(start, chunk), :] = o[k * chunk:(k + 1) * chunk, :]
            l1_s[r4 + d4 * k, pl.ds(start, chunk), :] = lb[k * chunk:(k + 1) * chunk, :]

    def g1_class(r4, carry):
        o, lse = _band_block(gather4(q1_ref, r4, 0), None, gather4(kp_ref, r4, 0), None,
                             gather4(vp_ref, r4, 0), None, mask4_cur)
        scatter4(o, lse, r4, 0)

        def g1_body(a, c2):
            r0 = pl.multiple_of(a * chunk, chunk)
            rp = pl.multiple_of((a - 1) * chunk, chunk)
            o, lse = _band_block(gather4(q1_ref, r4, r0),
                                 gather4(kp_ref, r4, rp), gather4(kp_ref, r4, r0),
                                 gather4(vp_ref, r4, rp), gather4(vp_ref, r4, r0),
                                 mask4_prev, mask4_cur)
            scatter4(o, lse, r4, r0)
            return c2

        lax.fori_loop(1, steps // chunk, g1_body, 0)
        return carry

    lax.fori_loop(0, d4, g1_class, 0)

    def merge_body(r, carry):
        o0 = o0_s[pl.ds(r, steps, stride=N_RES), :]
        l0 = l0_s[pl.ds(r, steps, stride=N_RES), :]
        o1, l1 = o1_s[r], l1_s[r]
        o2, l2 = o2_s[r], l2_s[r]
        lmax = jnp.maximum(jnp.maximum(l0, l1), l2)
        w0, w1, w2 = jnp.exp(l0 - lmax), jnp.exp(l1 - lmax), jnp.exp(l2 - lmax)
        inv = 1.0 / (w0 + w1 + w2)
        merged = (w0 * inv) * o0 + (w1 * inv) * o1 + (w2 * inv) * o2
        out_ref[r] = (merged * _silu(gate_ref[r].astype(F32))).astype(out_ref.dtype)
        return carry

    lax.fori_loop(0, N_RES, merge_body, 0)


def _attn_call(q0, kv_nat, q12g, kv_perm, *, n_heads, name):
    b, seq, e = q0.shape
    steps = seq // N_RES
    nat = lambda off: pl.BlockSpec((None, seq, HEAD_DIM), lambda bi, h: (bi, 0, h + off))
    res = lambda off: pl.BlockSpec((None, N_RES, steps, HEAD_DIM), lambda bi, h: (bi, 0, 0, h + off))
    res_scratch = pltpu.VMEM((N_RES, steps, LANES), F32)
    nat_scratch = pltpu.VMEM((seq, LANES), F32)
    return pl.pallas_call(
        _attn_kernel,
        grid=(b, n_heads),
        in_specs=[nat(0), nat(0), nat(n_heads),
                  res(0), res(n_heads), res(2 * n_heads),
                  res(0), res(n_heads)],
        out_specs=res(0),
        out_shape=jax.ShapeDtypeStruct((b, N_RES, steps, e), BF16),
        scratch_shapes=[nat_scratch, nat_scratch, res_scratch, res_scratch, res_scratch, res_scratch],
        compiler_params=_cparams(2),
        name=name,
    )(q0, kv_nat, kv_nat, q12g, q12g, q12g, kv_perm, kv_perm)


def _rope_tables(seq):
    inv_freq = 1.0 / (ROPE_THETA ** (jnp.arange(0, HEAD_DIM, 2, dtype=F32) / HEAD_DIM))
    ang = jnp.arange(seq, dtype=F32)[:, None] * inv_freq[None, :]
    cos, sin = jnp.cos(ang), jnp.sin(ang)
    return jnp.concatenate([cos, cos], axis=-1), jnp.concatenate([-sin, sin], axis=-1)


def _to_residue_order(table, seq):
    return table.reshape(seq // N_RES, N_RES, -1).transpose(1, 0, 2).reshape(seq, -1)


def _perm_matrix():
    idx = jnp.arange(PERM_ROWS)
    src = (idx % N_RES) * N_RES + idx // N_RES
    return (src[:, None] == idx[None, :]).astype(BF16)


def kernel(x, norm_a, w_in_a, w_grp_a, scale_a, w_out_a, norm_kv, w_k, w_v, norm_b, w_in_b,
           w_out_b, norm_f):
    b, seq, d = x.shape
    n = b * seq
    n_a = w_in_a.shape[0]
    n_b = w_in_b.shape[0]
    e_b = w_k.shape[1]
    n_heads = e_b // HEAD_DIM
    assert seq % (N_RES * BAND) == 0 and d % HEAD_DIM == 0

    cos_n, sin_n = _rope_tables(seq)
    cos_r, sin_r = _to_residue_order(cos_n, seq), _to_residue_order(sin_n, seq)
    pmat = _perm_matrix()

    xf = x.reshape(n, d)
    hdn = _rmsnorm_call(xf, norm_a[0])

    for i in range(n_a):
        e_a = w_in_a.shape[2] // 2
        proj = _proj_call(hdn, w_in_a[i].astype(BF16), cos_n, sin_n, pmat, seq=seq, col_off=0,
                          n_cols=2 * e_a, n_rope_cols=0, perm_out=False, name=f"a{i}_in")
        if i < n_a - 1:
            gains = norm_a[i + 1][None]
            specs = [(False, BF16)]
        else:
            gains = jnp.stack([norm_kv, norm_b[0], norm_b[0]])
            specs = [(False, BF16), (False, BF16), (True, BF16)]
        outs = _amix_call(proj, xf, w_grp_a[i].astype(BF16), scale_a[i], w_out_a[i].astype(BF16),
                          gains, pmat, seq=seq, norm_specs=specs, name=f"a{i}_mix")
        xf = outs[0]
        if i < n_a - 1:
            hdn = outs[1]
        else:
            hdn_kv, hdn_nat, hdn_res = outs[1], outs[2], outs[3]

    w_kv = jnp.concatenate([w_k, w_v], axis=1).astype(BF16)
    kv_nat, kv_res = _proj_call(hdn_kv, w_kv, cos_n, sin_n, pmat, seq=seq, col_off=0,
                                n_cols=2 * e_b, n_rope_cols=e_b, perm_out=True, name="kv")
    kv_nat = kv_nat.reshape(b, seq, 2 * e_b)

    out = None
    for i in range(n_b):
        w_in = w_in_b[i].astype(BF16)
        q0 = _proj_call(hdn_nat, w_in, cos_n, sin_n, pmat, seq=seq, col_off=0, n_cols=e_b,
                        n_rope_cols=e_b, perm_out=False, name=f"b{i}_q0")
        q12g = _proj_call(hdn_res.reshape(n, d), w_in, cos_r, sin_r, pmat, seq=seq, col_off=e_b,
                          n_cols=3 * e_b, n_rope_cols=2 * e_b, perm_out=False, name=f"b{i}_q12g")
        hg = _attn_call(q0.reshape(b, seq, e_b), kv_nat,
                        q12g.reshape(b, N_RES, seq // N_RES, 3 * e_b), kv_res,
                        n_heads=n_heads, name=f"b{i}_attn")
        if i < n_b - 1:
            gains = jnp.stack([norm_b[i + 1], norm_b[i + 1]])
            xf, hdn_nat, hdn_res = _bout_call(
                hg, xf, w_out_b[i].astype(BF16), gains, pmat, seq=seq, emit_x=True,
                norm_specs=[(False, BF16), (True, BF16)], name=f"b{i}_out")
        else:
            (out,) = _bout_call(hg, xf, w_out_b[i].astype(BF16), norm_f[None], pmat, seq=seq,
                                emit_x=False, norm_specs=[(False, F32)], name=f"b{i}_out")
    return out.reshape(b, seq, d)
```

```python
import functools
import math

import jax
import jax.numpy as jnp
from jax import lax
from jax.experimental import pallas as pl
from jax.experimental.pallas import tpu as pltpu

F32 = jnp.float32
BF16 = jnp.bfloat16

RMS_EPS = 1e-6
POOL_WINDOWS = (2, 4, 8, 16)
POOL_HALO = 16
HEAD_DIM = 128
ROPE_THETA = 10000.0
NEG_INF = -1e30
N_RES = 16
DILATIONS = (1, 4, 16)
BAND = 128
PERM_ROWS = N_RES * N_RES
LANES = 128
VMEM_LIMIT = 56 * 1024 * 1024


def _cparams(n_axes):
    return pltpu.CompilerParams(
        dimension_semantics=("arbitrary",) * n_axes, vmem_limit_bytes=VMEM_LIMIT)


def _silu(g):
    return g / (1.0 + jnp.exp(-g))


def _rmsnorm_kernel(x_ref, g_ref, o_ref):
    xf = x_ref[...]
    inv = lax.rsqrt(jnp.mean(xf * xf, axis=-1, keepdims=True) + RMS_EPS)
    o_ref[...] = ((xf * inv) * g_ref[...]).astype(o_ref.dtype)


def _rmsnorm_call(x, g, bm=512):
    n, d = x.shape
    return pl.pallas_call(
        _rmsnorm_kernel,
        grid=(n // bm,),
        in_specs=[pl.BlockSpec((bm, d), lambda i: (i, 0)),
                  pl.BlockSpec((1, d), lambda i: (0, 0))],
        out_specs=pl.BlockSpec((bm, d), lambda i: (i, 0)),
        out_shape=jax.ShapeDtypeStruct((n, d), BF16),
        compiler_params=_cparams(1),
        name="rmsnorm_first",
    )(x, g.reshape(1, d))


def _emit_norms(xn, gains_ref, p_ref, out_refs, norm_specs, bm):
    inv = lax.rsqrt(jnp.mean(xn * xn, axis=-1, keepdims=True) + RMS_EPS)
    xh = xn * inv
    for k, (permuted, dtype) in enumerate(norm_specs):
        hd = (xh * gains_ref[k:k + 1, :]).astype(dtype)
        if not permuted:
            out_refs[k][...] = hd
            continue
        steps = PERM_ROWS // N_RES
        for sub in range(bm // PERM_ROWS):
            pb = jnp.dot(p_ref[...], hd[sub * PERM_ROWS:(sub + 1) * PERM_ROWS, :],
                         preferred_element_type=F32).astype(dtype)
            for r in range(N_RES):
                out_refs[k][r, sub * steps:(sub + 1) * steps, :] = pb[r * steps:(r + 1) * steps, :]


def _proj_kernel(lhs_ref, w_ref, cos_ref, sin_ref, p_ref, o_ref, *perm_refs,
                 bm, bn, n_rope_blocks, n_col_blocks):
    j = pl.program_id(0)
    acc = jnp.dot(lhs_ref[...], w_ref[...], preferred_element_type=F32)

    def _rope():
        cos = cos_ref[...]
        sin = sin_ref[...]
        for hh in range(bn // HEAD_DIM):
            t = acc[:, hh * HEAD_DIM:(hh + 1) * HEAD_DIM]
            rot = pltpu.roll(t, HEAD_DIM // 2, 1)
            o_ref[:, hh * HEAD_DIM:(hh + 1) * HEAD_DIM] = (t * cos + rot * sin).astype(o_ref.dtype)

    def _plain():
        o_ref[...] = acc.astype(o_ref.dtype)

    if n_rope_blocks == 0:
        _plain()
    elif n_rope_blocks == n_col_blocks:
        _rope()
    else:
        pl.when(j < n_rope_blocks)(_rope)
        pl.when(j >= n_rope_blocks)(_plain)

    if perm_refs:
        op_ref = perm_refs[0]
        steps = PERM_ROWS // N_RES
        for sub in range(bm // PERM_ROWS):
            blk = o_ref[sub * PERM_ROWS:(sub + 1) * PERM_ROWS, :]
            pb = jnp.dot(p_ref[...], blk, preferred_element_type=F32).astype(op_ref.dtype)
            for r in range(N_RES):
                op_ref[r, sub * steps:(sub + 1) * steps, :] = pb[r * steps:(r + 1) * steps, :]


def _proj_call(lhs, w, cos, sin, perm_mat, *, seq, col_off, n_cols, n_rope_cols,
               perm_out, name, bm=512, bn=1024):
    n, k = lhs.shape
    n_col_blocks = n_cols // bn
    blocks_per_seq = seq // bm
    col_blk0 = col_off // bn
    kern = functools.partial(_proj_kernel, bm=bm, bn=bn,
                             n_rope_blocks=n_rope_cols // bn, n_col_blocks=n_col_blocks)
    out_shape = [jax.ShapeDtypeStruct((n, n_cols), BF16)]
    out_specs = [pl.BlockSpec((bm, bn), lambda j, i: (i, j))]
    if perm_out:
        out_shape.append(jax.ShapeDtypeStruct((n // seq, N_RES, seq // N_RES, n_cols), BF16))
        out_specs.append(pl.BlockSpec(
            (None, N_RES, bm // N_RES, bn),
            lambda j, i: (i // blocks_per_seq, 0, i % blocks_per_seq, j)))
    res = pl.pallas_call(
        kern,
        grid=(n_col_blocks, n // bm),
        in_specs=[pl.BlockSpec((bm, k), lambda j, i: (i, 0)),
                  pl.BlockSpec((k, bn), lambda j, i: (0, j + col_blk0)),
                  pl.BlockSpec((bm, HEAD_DIM), lambda j, i: (i % blocks_per_seq, 0)),
                  pl.BlockSpec((bm, HEAD_DIM), lambda j, i: (i % blocks_per_seq, 0)),
                  pl.BlockSpec((PERM_ROWS, PERM_ROWS), lambda j, i: (0, 0))],
        out_specs=out_specs,
        out_shape=out_shape,
        compiler_params=_cparams(2),
        name=name,
    )(lhs, w, cos, sin, perm_mat)
    return res if perm_out else res[0]


def _amix_kernel(u_ref, g_ref, halo_ref, x_ref, wgrp_ref, scale_ref, wout_ref, gains_ref,
                 p_ref, xo_ref, *rest, bm, blocks_per_seq, norm_specs):
    out_refs = rest[:len(norm_specs)]
    ext_ref, h_ref = rest[len(norm_specs):]
    blk = pl.program_id(0) % blocks_per_seq
    gc = u_ref.shape[1] // len(POOL_WINDOWS)

    halo = halo_ref[...].astype(F32)
    ext_ref[0:POOL_HALO, :] = jnp.where(blk == 0, 0.0, halo)
    ext_ref[POOL_HALO:, :] = u_ref[...].astype(F32)

    pos = blk * bm + lax.broadcasted_iota(jnp.int32, (bm, 1), 0)
    for g, w in enumerate(POOL_WINDOWS):
        cols = slice(g * gc, (g + 1) * gc)
        cur = ext_ref[POOL_HALO:POOL_HALO + bm, cols]
        wsum = cur
        for back in range(1, w):
            wsum = wsum + ext_ref[POOL_HALO - back:POOL_HALO - back + bm, cols]
        inv_cnt = 1.0 / jnp.minimum(pos + 1, w).astype(F32)
        pooled = wsum * inv_cnt - cur
        y = jnp.dot(pooled.astype(BF16), wgrp_ref[g], preferred_element_type=F32)
        y = y * scale_ref[:, cols]
        h_ref[:, cols] = (y * _silu(g_ref[:, cols].astype(F32))).astype(BF16)

    xn = x_ref[...] + jnp.dot(h_ref[...], wout_ref[...], preferred_element_type=F32)
    xo_ref[...] = xn
    _emit_norms(xn, gains_ref, p_ref, out_refs, norm_specs, bm)


def _norm_out(n, d, seq, bm, permuted, dtype):
    blocks_per_seq = seq // bm
    if permuted:
        shape = jax.ShapeDtypeStruct((n // seq, N_RES, seq // N_RES, d), dtype)
        spec = pl.BlockSpec((None, N_RES, bm // N_RES, d),
                            lambda i: (i // blocks_per_seq, 0, i % blocks_per_seq, 0))
    else:
        shape = jax.ShapeDtypeStruct((n, d), dtype)
        spec = pl.BlockSpec((bm, d), lambda i: (i, 0))
    return shape, spec


def _const_spec(shape):
    zeros = (0,) * len(shape)
    return pl.BlockSpec(shape, lambda i: zeros, pipeline_mode=pl.Buffered(1))


def _amix_call(proj, x, w_grp, scale, w_out, gains, perm_mat, *, seq, norm_specs, name, bm=256):
    n, d = x.shape
    e = proj.shape[1] // 2
    blocks_per_seq = seq // bm
    halo_per_blk = bm // POOL_HALO
    norm_shapes, norm_out_specs = zip(*[_norm_out(n, d, seq, bm, p, dt) for p, dt in norm_specs])
    kern = functools.partial(_amix_kernel, bm=bm, blocks_per_seq=blocks_per_seq,
                             norm_specs=tuple(norm_specs))
    return pl.pallas_call(
        kern,
        grid=(n // bm,),
        in_specs=[pl.BlockSpec((bm, e), lambda i: (i, 0)),
                  pl.BlockSpec((bm, e), lambda i: (i, 1)),
                  pl.BlockSpec((POOL_HALO, e), lambda i: (jnp.maximum(i * halo_per_blk - 1, 0), 0)),
                  pl.BlockSpec((bm, d), lambda i: (i, 0)),
                  _const_spec(w_grp.shape),
                  _const_spec((1, e)),
                  _const_spec(w_out.shape),
                  _const_spec(gains.shape),
                  _const_spec(perm_mat.shape)],
        out_specs=[pl.BlockSpec((bm, d), lambda i: (i, 0))] + list(norm_out_specs),
        out_shape=[jax.ShapeDtypeStruct((n, d), F32)] + list(norm_shapes),
        scratch_shapes=[pltpu.VMEM((bm + POOL_HALO, e), F32), pltpu.VMEM((bm, e), BF16)],
        compiler_params=_cparams(1),
        name=name,
    )(proj, proj, proj, x, w_grp, scale.reshape(1, e), w_out, gains, perm_mat)


def _bout_kernel(hg_ref, x_ref, wout_ref, gains_ref, p_ref, *rest, bm, emit_x, norm_specs):
    n_out = len(norm_specs) + (1 if emit_x else 0)
    outs = rest[:n_out]
    (h_ref,) = rest[n_out:]
    steps = PERM_ROWS // N_RES
    for sub in range(bm // PERM_ROWS):
        blk = jnp.concatenate(
            [hg_ref[r, sub * steps:(sub + 1) * steps, :] for r in range(N_RES)], axis=0)
        h_ref[sub * PERM_ROWS:(sub + 1) * PERM_ROWS, :] = jnp.dot(
            p_ref[...], blk, preferred_element_type=F32).astype(BF16)
    xn = x_ref[...] + jnp.dot(h_ref[...], wout_ref[...], preferred_element_type=F32)
    if emit_x:
        outs[0][...] = xn
        outs = outs[1:]
    _emit_norms(xn, gains_ref, p_ref, outs, norm_specs, bm)


def _bout_call(hg, x, w_out, gains, perm_mat, *, seq, emit_x, norm_specs, name, bm=256):
    n, d = x.shape
    e = hg.shape[-1]
    blocks_per_seq = seq // bm
    norm_shapes, norm_out_specs = zip(*[_norm_out(n, d, seq, bm, p, dt) for p, dt in norm_specs])
    out_specs, out_shape = list(norm_out_specs), list(norm_shapes)
    if emit_x:
        out_specs.insert(0, pl.BlockSpec((bm, d), lambda i: (i, 0)))
        out_shape.insert(0, jax.ShapeDtypeStruct((n, d), F32))
    kern = functools.partial(_bout_kernel, bm=bm, emit_x=emit_x, norm_specs=tuple(norm_specs))
    return pl.pallas_call(
        kern,
        grid=(n // bm,),
        in_specs=[pl.BlockSpec((None, N_RES, bm // N_RES, e),
                               lambda i: (i // blocks_per_seq, 0, i % blocks_per_seq, 0)),
                  pl.BlockSpec((bm, d), lambda i: (i, 0)),
                  _const_spec(w_out.shape),
                  _const_spec(gains.shape),
                  _const_spec(perm_mat.shape)],
        out_specs=out_specs,
        out_shape=out_shape,
        scratch_shapes=[pltpu.VMEM((bm, e), BF16)],
        compiler_params=_cparams(1),
        name=name,
    )(hg, x, w_out, gains, perm_mat)


def _band_blocks(blocks):
    scale = 1.0 / math.sqrt(HEAD_DIM)
    nt = (((1,), (1,)), ((), ()))
    ss = [lax.dot_general(q, k, nt, preferred_element_type=F32) * scale for q, k, _, _ in blocks]
    ss = [jnp.where(blk[3], s, NEG_INF) for blk, s in zip(blocks, ss)]
    ms = [jnp.max(s, axis=-1, keepdims=True) for s in ss]
    ps = [jnp.exp(s - m).astype(BF16) for s, m in zip(ss, ms)]
    pvs = [jnp.dot(p, jnp.concatenate([blk[2], jnp.ones_like(blk[2])], axis=1),
                   preferred_element_type=F32) for blk, p in zip(blocks, ps)]
    return [(pv[:, :HEAD_DIM], m, pv[:, HEAD_DIM:]) for pv, m in zip(pvs, ms)]


def _attn_kernel(q0_ref, kn_ref, vn_ref, q1_ref, q2_ref, gate_ref, kp_ref, vp_ref, out_ref,
                 a0_s, m0_s, l0_s, a1_s, m1_s, l1_s, a2_s, m2_s, l2_s):
    seq = q0_ref.shape[0]
    steps = seq // N_RES
    row = lax.broadcasted_iota(jnp.int32, (BAND, 2 * BAND), 0)
    col = lax.broadcasted_iota(jnp.int32, (BAND, 2 * BAND), 1)
    mask_band = (col >= row) & (col <= row + BAND)
    row1 = lax.broadcasted_iota(jnp.int32, (BAND, BAND), 0)
    col1 = lax.broadcasted_iota(jnp.int32, (BAND, BAND), 1)
    mask_first = col1 <= row1
    d4 = DILATIONS[1]
    chunk = BAND // d4
    n_sub = N_RES // d4
    row4 = d4 * (row % chunk) + row // chunk
    colb = col % BAND
    col4 = d4 * (colb % chunk) + colb // chunk + BAND * (col // BAND)
    mask4_band = (col4 >= row4) & (col4 <= row4 + BAND)
    mask4_first = (d4 * (col1 % chunk) + col1 // chunk) <= (d4 * (row1 % chunk) + row1 // chunk)

    def bcast(m):
        return jnp.broadcast_to(m, (m.shape[0], LANES))

    unroll0 = 4
    span = unroll0 * BAND

    def g0_group(base, first):
        q_all = q0_ref[pl.ds(base, span), :]
        if first:
            k_all, v_all = kn_ref[0:span, :], vn_ref[0:span, :]
        else:
            lo = pl.multiple_of(base - BAND, BAND)
            k_all, v_all = kn_ref[pl.ds(lo, span + BAND), :], vn_ref[pl.ds(lo, span + BAND), :]
        blocks = []
        for u in range(unroll0):
            q = q_all[u * BAND:(u + 1) * BAND, :]
            if first and u == 0:
                blocks.append((q, k_all[0:BAND, :], v_all[0:BAND, :], mask_first))
            else:
                k0 = (u - 1) * BAND if first else u * BAND
                blocks.append((q, k_all[k0:k0 + 2 * BAND, :], v_all[k0:k0 + 2 * BAND, :], mask_band))
        for u, (acc, m, l) in enumerate(_band_blocks(blocks)):
            rows = pl.ds(pl.multiple_of(base + u * BAND, BAND), BAND)
            a0_s[rows, :] = acc
            m0_s[rows, :] = bcast(m)
            l0_s[rows, :] = l

    g0_group(0, True)

    def g0_body(it, carry):
        g0_group(pl.multiple_of(it * span, span), False)
        return carry

    lax.fori_loop(1, seq // span, g0_body, 0)

    unroll2 = 2
    n_blk2 = steps // BAND

    def g2_body(it, carry):
        blocks = []
        for c in range(unroll2):
            r = it * unroll2 + c
            q_all, k_all, v_all = q2_ref[r], kp_ref[r], vp_ref[r]
            for nb in range(n_blk2):
                q = q_all[nb * BAND:(nb + 1) * BAND, :]
                if nb == 0:
                    blocks.append((q, k_all[0:BAND, :], v_all[0:BAND, :], mask_first))
                else:
                    k0 = (nb - 1) * BAND
                    blocks.append((q, k_all[k0:k0 + 2 * BAND, :], v_all[k0:k0 + 2 * BAND, :], mask_band))
        res = _band_blocks(blocks)
        for c in range(unroll2):
            r = it * unroll2 + c
            for nb in range(n_blk2):
                acc, m, l = res[c * n_blk2 + nb]
                rows = slice(nb * BAND, (nb + 1) * BAND)
                a2_s[r, rows, :] = acc
                m2_s[r, rows, :] = bcast(m)
                l2_s[r, rows, :] = l
        return carry

    lax.fori_loop(0, N_RES // unroll2, g2_body, 0)

    n_blk1 = steps // chunk
    unroll1 = 4

    def g1_body(it, carry):
        r4 = it // (n_blk1 // unroll1)
        half = it % (n_blk1 // unroll1)
        a_lo = half * unroll1
        q_c, k_c, v_c = q1_ref, kp_ref, vp_ref

        def rows_of(ref, a):
            return [ref[r4 + d4 * k, a * chunk:(a + 1) * chunk, :] for k in range(n_sub)]

        def build(first_half):
            blocks = []
            for u in range(unroll1):
                a = u if first_half else unroll1 + u
                q = jnp.concatenate(rows_of(q_c, a), axis=0)
                if a == 0:
                    blocks.append((q, jnp.concatenate(rows_of(k_c, 0), axis=0),
                                   jnp.concatenate(rows_of(v_c, 0), axis=0), mask4_first))
                else:
                    blocks.append((q,
                                   jnp.concatenate(rows_of(k_c, a - 1) + rows_of(k_c, a), axis=0),
                                   jnp.concatenate(rows_of(v_c, a - 1) + rows_of(v_c, a), axis=0),
                                   mask4_band))
            for u, (acc, m, l) in enumerate(_band_blocks(blocks)):
                a = u if first_half else unroll1 + u
                mb = bcast(m)
                for k in range(n_sub):
                    dst = (r4 + d4 * k, slice(a * chunk, (a + 1) * chunk), slice(None))
                    src = slice(k * chunk, (k + 1) * chunk)
                    a1_s[dst] = acc[src, :]
                    m1_s[dst] = mb[src, :]
                    l1_s[dst] = l[src, :]

        pl.when(half == 0)(functools.partial(build, True))
        pl.when(half != 0)(functools.partial(build, False))
        return carry

    assert n_blk1 == 2 * unroll1
    lax.fori_loop(0, d4 * (n_blk1 // unroll1), g1_body, 0)

    def merge_body(r, carry):
        nat_rows = pl.ds(r, steps, stride=N_RES)
        a0, m0, l0 = a0_s[nat_rows, :], m0_s[nat_rows, :], l0_s[nat_rows, :]
        a1, m1, l1 = a1_s[r], m1_s[r], l1_s[r]
        a2, m2, l2 = a2_s[r], m2_s[r], l2_s[r]
        mx = jnp.maximum(jnp.maximum(m0, m1), m2)
        w0, w1, w2 = jnp.exp(m0 - mx), jnp.exp(m1 - mx), jnp.exp(m2 - mx)
        num = w0 * a0 + w1 * a1 + w2 * a2
        den = w0 * l0 + w1 * l1 + w2 * l2
        out_ref[r] = ((num / den) * _silu(gate_ref[r].astype(F32))).astype(out_ref.dtype)
        return carry

    lax.fori_loop(0, N_RES, merge_body, 0)


def _attn_call(q0, kv_nat, q12g, kv_perm, *, n_heads, name):
    b, seq, e = q0.shape
    steps = seq // N_RES
    nat = lambda off: pl.BlockSpec((None, seq, HEAD_DIM), lambda bi, h: (bi, 0, h + off))
    res = lambda off: pl.BlockSpec((None, N_RES, steps, HEAD_DIM), lambda bi, h: (bi, 0, 0, h + off))
    res_scratch = pltpu.VMEM((N_RES, steps, LANES), F32)
    nat_scratch = pltpu.VMEM((seq, LANES), F32)
    return pl.pallas_call(
        _attn_kernel,
        grid=(b, n_heads),
        in_specs=[nat(0), nat(0), nat(n_heads),
                  res(0), res(n_heads), res(2 * n_heads),
                  res(0), res(n_heads)],
        out_specs=res(0),
        out_shape=jax.ShapeDtypeStruct((b, N_RES, steps, e), BF16),
        scratch_shapes=[nat_scratch] * 3 + [res_scratch] * 6,
        compiler_params=_cparams(2),
        name=name,
    )(q0, kv_nat, kv_nat, q12g, q12g, q12g, kv_perm, kv_perm)


def _rope_tables(seq):
    inv_freq = 1.0 / (ROPE_THETA ** (jnp.arange(0, HEAD_DIM, 2, dtype=F32) / HEAD_DIM))
    ang = jnp.arange(seq, dtype=F32)[:, None] * inv_freq[None, :]
    cos, sin = jnp.cos(ang), jnp.sin(ang)
    return jnp.concatenate([cos, cos], axis=-1), jnp.concatenate([-sin, sin], axis=-1)


def _to_residue_order(table, seq):
    return table.reshape(seq // N_RES, N_RES, -1).transpose(1, 0, 2).reshape(seq, -1)


def _perm_matrix():
    idx = jnp.arange(PERM_ROWS)
    src = (idx % N_RES) * N_RES + idx // N_RES
    return (src[:, None] == idx[None, :]).astype(BF16)


def kernel(x, norm_a, w_in_a, w_grp_a, scale_a, w_out_a, norm_kv, w_k, w_v, norm_b, w_in_b,
           w_out_b, norm_f):
    b, seq, d = x.shape
    n = b * seq
    n_a = w_in_a.shape[0]
    n_b = w_in_b.shape[0]
    e_b = w_k.shape[1]
    n_heads = e_b // HEAD_DIM
    assert seq % (N_RES * BAND) == 0 and d % HEAD_DIM == 0

    cos_n, sin_n = _rope_tables(seq)
    cos_r, sin_r = _to_residue_order(cos_n, seq), _to_residue_order(sin_n, seq)
    pmat = _perm_matrix()

    xf = x.reshape(n, d)
    hdn = _rmsnorm_call(xf, norm_a[0])

    for i in range(n_a):
        e_a = w_in_a.shape[2] // 2
        proj = _proj_call(hdn, w_in_a[i].astype(BF16), cos_n, sin_n, pmat, seq=seq, col_off=0,
                          n_cols=2 * e_a, n_rope_cols=0, perm_out=False, name=f"a{i}_in")
        if i < n_a - 1:
            gains = norm_a[i + 1][None]
            specs = [(False, BF16)]
        else:
            gains = jnp.stack([norm_kv, norm_b[0], norm_b[0]])
            specs = [(False, BF16), (False, BF16), (True, BF16)]
        outs = _amix_call(proj, xf, w_grp_a[i].astype(BF16), scale_a[i], w_out_a[i].astype(BF16),
                          gains, pmat, seq=seq, norm_specs=specs, name=f"a{i}_mix")
        xf = outs[0]
        if i < n_a - 1:
            hdn = outs[1]
        else:
            hdn_kv, hdn_nat, hdn_res = outs[1], outs[2], outs[3]

    w_kv = jnp.concatenate([w_k, w_v], axis=1).astype(BF16)
    kv_nat, kv_res = _proj_call(hdn_kv, w_kv, cos_n, sin_n, pmat, seq=seq, col_off=0,
                                n_cols=2 * e_b, n_rope_cols=e_b, perm_out=True, name="kv")
    kv_nat = kv_nat.reshape(b, seq, 2 * e_b)

    out = None
    for i in range(n_b):
        w_in = w_in_b[i].astype(BF16)
        q0 = _proj_call(hdn_nat, w_in, cos_n, sin_n, pmat, seq=seq, col_off=0, n_cols=e_b,
                        n_rope_cols=e_b, perm_out=False, name=f"b{i}_q0")
        q12g = _proj_call(hdn_res.reshape(n, d), w_in, cos_r, sin_r, pmat, seq=seq, col_off=e_b,
                          n_cols=3 * e_b, n_rope_cols=2 * e_b, perm_out=False, name=f"b{i}_q12g")
        hg = _attn_call(q0.reshape(b, seq, e_b), kv_nat,
                        q12g.reshape(b, N_RES, seq // N_RES, 3 * e_b), kv_res,
                        n_heads=n_heads, name=f"b{i}_attn")
        if i < n_b - 1:
            gains = jnp.stack([norm_b[i + 1], norm_b[i + 1]])
            xf, hdn_nat, hdn_res = _bout_call(
                hg, xf, w_out_b[i].astype(BF16), gains, pmat, seq=seq, emit_x=True,
                norm_specs=[(False, BF16), (True, BF16)], name=f"b{i}_out")
        else:
            (out,) = _bout_call(hg, xf, w_out_b[i].astype(BF16), norm_f[None], pmat, seq=seq,
                                emit_x=False, norm_specs=[(False, F32)], name=f"b{i}_out")
    return out.reshape(b, seq, d)
```

```python
import functools
import math

import jax
import jax.numpy as jnp
from jax import lax
from jax.experimental import pallas as pl
from jax.experimental.pallas import tpu as pltpu

F32 = jnp.float32
BF16 = jnp.bfloat16

RMS_EPS = 1e-6
POOL_WINDOWS = (2, 4, 8, 16)
POOL_HALO = 16
HEAD_DIM = 128
ROPE_THETA = 10000.0
NEG_INF = -1e30
N_RES = 16
DILATIONS = (1, 4, 16)
BAND = 128
SUBLANES = 8
LANES = 128
RES_ROWS = N_RES * N_RES
BLK_ROWS = BAND
VMEM_LIMIT = 56 * 1024 * 1024


def _cparams(n_axes):
    return pltpu.CompilerParams(
        dimension_semantics=("arbitrary",) * n_axes, vmem_limit_bytes=VMEM_LIMIT)


def _const_spec(shape, n_grid, layer=None):
    if layer is None:
        block, index = shape, (0,) * len(shape)
    else:
        block, index = (None,) + tuple(shape[1:]), (layer,) + (0,) * (len(shape) - 1)
    if n_grid == 1:
        return pl.BlockSpec(block, lambda i: index, pipeline_mode=pl.Buffered(1))
    return pl.BlockSpec(block, lambda j, i: index, pipeline_mode=pl.Buffered(1))


def _store_res(dst_ref, p_ref, rows, bm):
    steps = RES_ROWS // N_RES
    for sub in range(bm // RES_ROWS):
        pb = jnp.dot(p_ref[...], rows[sub * RES_ROWS:(sub + 1) * RES_ROWS, :],
                     preferred_element_type=F32).astype(dst_ref.dtype)
        for r in range(N_RES):
            dst_ref[r, sub * steps:(sub + 1) * steps, :] = pb[r * steps:(r + 1) * steps, :]


def _rmsnorm_kernel(x_ref, g_ref, o_ref):
    xf = x_ref[...]
    inv = lax.rsqrt(jnp.mean(xf * xf, axis=-1, keepdims=True) + RMS_EPS)
    o_ref[...] = ((xf * inv) * g_ref[...]).astype(o_ref.dtype)


def _rmsnorm_call(x, g, bm=512):
    n, d = x.shape
    return pl.pallas_call(
        _rmsnorm_kernel,
        grid=(n // bm,),
        in_specs=[pl.BlockSpec((bm, d), lambda i: (i, 0)),
                  pl.BlockSpec((1, d), lambda i: (0, 0))],
        out_specs=pl.BlockSpec((bm, d), lambda i: (i, 0)),
        out_shape=jax.ShapeDtypeStruct((n, d), BF16),
        compiler_params=_cparams(1),
        name="rmsnorm_first",
    )(x, g.reshape(1, d))


def _emit_norms(xn, gains_ref, pres_ref, pblk_ref, out_refs, norm_specs, bm):
    inv = lax.rsqrt(jnp.mean(xn * xn, axis=-1, keepdims=True) + RMS_EPS)
    xh = xn * inv
    for k, (order, dtype) in enumerate(norm_specs):
        hd = (xh * gains_ref[k:k + 1, :]).astype(dtype)
        if order == "nat":
            out_refs[k][...] = hd
        elif order == "res":
            _store_res(out_refs[k], pres_ref, hd, bm)
        else:
            for sub in range(bm // BLK_ROWS):
                rows = slice(sub * BLK_ROWS, (sub + 1) * BLK_ROWS)
                out_refs[k][rows, :] = jnp.dot(pblk_ref[...], hd[rows, :],
                                               preferred_element_type=F32).astype(dtype)


def _norm_out(n, d, seq, bm, order, dtype):
    blocks_per_seq = seq // bm
    if order == "res":
        shape = jax.ShapeDtypeStruct((n // seq, N_RES, seq // N_RES, d), dtype)
        spec = pl.BlockSpec((None, N_RES, bm // N_RES, d),
                            lambda i: (i // blocks_per_seq, 0, i % blocks_per_seq, 0))
    else:
        shape = jax.ShapeDtypeStruct((n, d), dtype)
        spec = pl.BlockSpec((bm, d), lambda i: (i, 0))
    return shape, spec


def _proj_kernel(*refs, bm, bn, rope, perm_out):
    refs = list(refs)
    lhs_ref, w_ref = refs[:2]
    del refs[:2]
    if rope:
        cos_ref, sin_ref = refs[:2]
        del refs[:2]
    if perm_out:
        p_ref = refs.pop(0)
    o_ref = refs.pop(0)
    wbf_ref = refs.pop()

    @pl.when(pl.program_id(1) == 0)
    def _():
        wbf_ref[...] = w_ref[...].astype(BF16)

    acc = jnp.dot(lhs_ref[...], wbf_ref[...], preferred_element_type=F32)
    if rope:
        cos = cos_ref[...]
        sin = sin_ref[...]
        for hh in range(bn // HEAD_DIM):
            t = acc[:, hh * HEAD_DIM:(hh + 1) * HEAD_DIM]
            rot = pltpu.roll(t, HEAD_DIM // 2, 1)
            o_ref[:, hh * HEAD_DIM:(hh + 1) * HEAD_DIM] = (t * cos + rot * sin).astype(o_ref.dtype)
    else:
        o_ref[...] = acc.astype(o_ref.dtype)
    if perm_out:
        _store_res(refs[0], p_ref, o_ref[...], bm)


def _proj_call(lhs, w, *, seq, col_off, n_cols, name, layer=0, tables=None, perm_mat=None,
               bm=512, bn=1024):
    n, k = lhs.shape
    blocks_per_seq = seq // bm
    col_blk0 = col_off // bn
    rope, perm_out = tables is not None, perm_mat is not None
    args = [lhs, w]
    in_specs = [pl.BlockSpec((bm, k), lambda j, i: (i, 0)),
                pl.BlockSpec((None, k, bn), lambda j, i: (layer, 0, j + col_blk0))]
    if rope:
        args += list(tables)
        in_specs += [pl.BlockSpec((bm, HEAD_DIM), lambda j, i: (i % blocks_per_seq, 0))] * 2
    if perm_out:
        args.append(perm_mat)
        in_specs.append(_const_spec(perm_mat.shape, 2))
    out_shape = [jax.ShapeDtypeStruct((n, n_cols), BF16)]
    out_specs = [pl.BlockSpec((bm, bn), lambda j, i: (i, j))]
    if perm_out:
        out_shape.append(jax.ShapeDtypeStruct((n // seq, N_RES, seq // N_RES, n_cols), BF16))
        out_specs.append(pl.BlockSpec(
            (None, N_RES, bm // N_RES, bn),
            lambda j, i: (i // blocks_per_seq, 0, i % blocks_per_seq, j)))
    res = pl.pallas_call(
        functools.partial(_proj_kernel, bm=bm, bn=bn, rope=rope, perm_out=perm_out),
        grid=(n_cols // bn, n // bm),
        in_specs=in_specs,
        out_specs=out_specs,
        out_shape=out_shape,
        scratch_shapes=[pltpu.VMEM((k, bn), BF16)],
        compiler_params=_cparams(2),
        name=name,
    )(*args)
    return res if perm_out else res[0]


def _amix_kernel(u_ref, g_ref, halo_ref, x_ref, wgrp_ref, scale_ref, wout_ref, gains_ref,
                 pres_ref, pblk_ref, xo_ref, *rest, bm, blocks_per_seq, norm_specs):
    out_refs = rest[:len(norm_specs)]
    ext_ref, h_ref = rest[len(norm_specs):]
    blk = pl.program_id(0) % blocks_per_seq
    gc = u_ref.shape[1] // len(POOL_WINDOWS)

    halo = halo_ref[...].astype(F32)
    ext_ref[0:POOL_HALO, :] = jnp.where(blk == 0, 0.0, halo)
    ext_ref[POOL_HALO:, :] = u_ref[...].astype(F32)

    pos = blk * bm + lax.broadcasted_iota(jnp.int32, (bm, 1), 0)
    for g, w in enumerate(POOL_WINDOWS):
        cols = slice(g * gc, (g + 1) * gc)
        ext = ext_ref[:, cols]
        wsum, have = ext, 1
        while have < w:
            wsum = wsum + pltpu.roll(wsum, have, 0)
            have *= 2
        cur = ext[POOL_HALO:, :]
        inv_cnt = 1.0 / jnp.minimum(pos + 1, w).astype(F32)
        pooled = wsum[POOL_HALO:, :] * inv_cnt - cur
        y = jnp.dot(pooled.astype(BF16), wgrp_ref[g], preferred_element_type=F32)
        y = y * scale_ref[:, cols]
        gt = g_ref[:, cols].astype(F32)
        h_ref[:, cols] = (y * (gt / (1.0 + jnp.exp(-gt)))).astype(BF16)

    xn = x_ref[...] + jnp.dot(h_ref[...], wout_ref[...], preferred_element_type=F32)
    xo_ref[...] = xn
    _emit_norms(xn, gains_ref, pres_ref, pblk_ref, out_refs, norm_specs, bm)


def _amix_call(proj, x, w_grp, scale, w_out, gains, pres, pblk, *, layer, seq, norm_specs, name,
               bm=256):
    n, d = x.shape
    e = proj.shape[1] // 2
    blocks_per_seq = seq // bm
    halo_per_blk = bm // POOL_HALO
    norm_shapes, norm_out_specs = zip(*[_norm_out(n, d, seq, bm, o, dt) for o, dt in norm_specs])
    kern = functools.partial(_amix_kernel, bm=bm, blocks_per_seq=blocks_per_seq,
                             norm_specs=tuple(norm_specs))
    return pl.pallas_call(
        kern,
        grid=(n // bm,),
        in_specs=[pl.BlockSpec((bm, e), lambda i: (i, 0)),
                  pl.BlockSpec((bm, e), lambda i: (i, 1)),
                  pl.BlockSpec((POOL_HALO, e), lambda i: (jnp.maximum(i * halo_per_blk - 1, 0), 0)),
                  pl.BlockSpec((bm, d), lambda i: (i, 0)),
                  _const_spec(w_grp.shape, 1, layer),
                  _const_spec((scale.shape[0], 1, e), 1, layer),
                  _const_spec(w_out.shape, 1, layer),
                  _const_spec(gains.shape, 1),
                  _const_spec(pres.shape, 1),
                  _const_spec(pblk.shape, 1)],
        out_specs=[pl.BlockSpec((bm, d), lambda i: (i, 0))] + list(norm_out_specs),
        out_shape=[jax.ShapeDtypeStruct((n, d), F32)] + list(norm_shapes),
        scratch_shapes=[pltpu.VMEM((bm + POOL_HALO, e), F32), pltpu.VMEM((bm, e), BF16)],
        compiler_params=_cparams(1),
        name=name,
    )(proj, proj, proj, x, w_grp, scale.reshape(scale.shape[0], 1, e), w_out, gains, pres, pblk)


def _bout_kernel(hg_ref, x_ref, wout_ref, gains_ref, pres_ref, pblk_ref, *rest,
                 bm, emit_x, norm_specs):
    n_out = len(norm_specs) + (1 if emit_x else 0)
    outs = rest[:n_out]
    (h_ref,) = rest[n_out:]
    steps = RES_ROWS // N_RES
    for sub in range(bm // RES_ROWS):
        blk = jnp.concatenate(
            [hg_ref[r, sub * steps:(sub + 1) * steps, :] for r in range(N_RES)], axis=0)
        h_ref[sub * RES_ROWS:(sub + 1) * RES_ROWS, :] = jnp.dot(
            pres_ref[...], blk, preferred_element_type=F32).astype(BF16)
    xn = x_ref[...] + jnp.dot(h_ref[...], wout_ref[...], preferred_element_type=F32)
    if emit_x:
        outs[0][...] = xn
        outs = outs[1:]
    _emit_norms(xn, gains_ref, pres_ref, pblk_ref, outs, norm_specs, bm)


def _bout_call(hg, x, w_out, gains, pres, pblk, *, layer, seq, emit_x, norm_specs, name, bm=256):
    n, d = x.shape
    e = hg.shape[-1]
    blocks_per_seq = seq // bm
    norm_shapes, norm_out_specs = zip(*[_norm_out(n, d, seq, bm, o, dt) for o, dt in norm_specs])
    out_specs, out_shape = list(norm_out_specs), list(norm_shapes)
    if emit_x:
        out_specs.insert(0, pl.BlockSpec((bm, d), lambda i: (i, 0)))
        out_shape.insert(0, jax.ShapeDtypeStruct((n, d), F32))
    kern = functools.partial(_bout_kernel, bm=bm, emit_x=emit_x, norm_specs=tuple(norm_specs))
    return pl.pallas_call(
        kern,
        grid=(n // bm,),
        in_specs=[pl.BlockSpec((None, N_RES, bm // N_RES, e),
                               lambda i: (i // blocks_per_seq, 0, i % blocks_per_seq, 0)),
                  pl.BlockSpec((bm, d), lambda i: (i, 0)),
                  _const_spec(w_out.shape, 1, layer),
                  _const_spec(gains.shape, 1),
                  _const_spec(pres.shape, 1),
                  _const_spec(pblk.shape, 1)],
        out_specs=out_specs,
        out_shape=out_shape,
        scratch_shapes=[pltpu.VMEM((bm, e), BF16)],
        compiler_params=_cparams(1),
        name=name,
    )(hg, x, w_out, gains, pres, pblk)


def _band_blocks(blocks):
    scale = 1.0 / math.sqrt(HEAD_DIM)
    nt = (((1,), (1,)), ((), ()))
    ss = [lax.dot_general(q, k, nt, preferred_element_type=F32) * scale for q, k, _, _ in blocks]
    ss = [jnp.where(blk[3], s, NEG_INF) for blk, s in zip(blocks, ss)]
    ms = [jnp.max(s, axis=-1, keepdims=True) for s in ss]
    ps = [jnp.exp(s - m).astype(BF16) for s, m in zip(ss, ms)]
    pvs = [jnp.dot(p, jnp.concatenate([blk[2], jnp.ones_like(blk[2])], axis=1),
                   preferred_element_type=F32) for blk, p in zip(blocks, ps)]
    return [(pv[:, :HEAD_DIM], m, pv[:, HEAD_DIM:]) for pv, m in zip(pvs, ms)]


def _band_masks(row_idx, col_idx, row_idx1, col_idx1):
    band = (col_idx >= row_idx) & (col_idx <= row_idx + BAND)
    return band, col_idx1 <= row_idx1


def _attn_kernel(q0_ref, kn_ref, vn_ref, q1_ref, q2_ref, gate_ref, kp_ref, vp_ref, out_ref,
                 a0_s, m0_s, l0_s, a1_s, m1_s, l1_s, a2_s, m2_s, l2_s):
    seq = q0_ref.shape[0]
    steps = seq // N_RES
    row = lax.broadcasted_iota(jnp.int32, (BAND, 2 * BAND), 0)
    col = lax.broadcasted_iota(jnp.int32, (BAND, 2 * BAND), 1)
    row1 = lax.broadcasted_iota(jnp.int32, (BAND, BAND), 0)
    col1 = lax.broadcasted_iota(jnp.int32, (BAND, BAND), 1)
    mask_band, mask_first = _band_masks(row, col, row1, col1)
    blk_pos = lambda p: N_RES * (p % SUBLANES) + p // SUBLANES
    mask0_band, mask0_first = _band_masks(blk_pos(row), col, blk_pos(row1), col1)
    d4 = DILATIONS[1]
    chunk = BAND // d4
    n_sub = N_RES // d4
    d4_pos = lambda p: d4 * (p % chunk) + p // chunk
    mask4_band, mask4_first = _band_masks(
        d4_pos(row), d4_pos(col % BAND) + BAND * (col // BAND), d4_pos(row1), d4_pos(col1))

    def bcast(m):
        return jnp.broadcast_to(m, (m.shape[0], LANES))

    unroll0 = 8
    span = unroll0 * BAND
    tile = (N_RES, SUBLANES, LANES)

    def g0_group(it, first):
        base = 0 if first else pl.multiple_of(it * span, span)
        q_all = q0_ref[pl.ds(base, span), :]
        if first:
            k_all, v_all = kn_ref[0:span, :], vn_ref[0:span, :]
        else:
            lo = pl.multiple_of(base - BAND, BAND)
            k_all, v_all = kn_ref[pl.ds(lo, span + BAND), :], vn_ref[pl.ds(lo, span + BAND), :]
        blocks = []
        for u in range(unroll0):
            q = q_all[u * BAND:(u + 1) * BAND, :]
            if first and u == 0:
                blocks.append((q, k_all[0:BAND, :], v_all[0:BAND, :], mask0_first))
            else:
                k0 = (u - 1) * BAND if first else u * BAND
                blocks.append((q, k_all[k0:k0 + 2 * BAND, :], v_all[k0:k0 + 2 * BAND, :], mask0_band))
        for u, (acc, m, l) in enumerate(_band_blocks(blocks)):
            a0_s[it * unroll0 + u] = acc.reshape(tile)
            m0_s[it * unroll0 + u] = bcast(m).reshape(tile)
            l0_s[it * unroll0 + u] = l.reshape(tile)

    g0_group(0, True)

    def g0_body(it, carry):
        g0_group(it, False)
        return carry

    lax.fori_loop(1, seq // span, g0_body, 0)

    unroll2 = 4
    n_blk2 = steps // BAND

    def g2_body(it, carry):
        blocks = []
        for c in range(unroll2):
            r = it * unroll2 + c
            q_all, k_all, v_all = q2_ref[r], kp_ref[r], vp_ref[r]
            for nb in range(n_blk2):
                q = q_all[nb * BAND:(nb + 1) * BAND, :]
                if nb == 0:
                    blocks.append((q, k_all[0:BAND, :], v_all[0:BAND, :], mask_first))
                else:
                    k0 = (nb - 1) * BAND
                    blocks.append((q, k_all[k0:k0 + 2 * BAND, :], v_all[k0:k0 + 2 * BAND, :], mask_band))
        res = _band_blocks(blocks)
        for c in range(unroll2):
            r = it * unroll2 + c
            for nb in range(n_blk2):
                acc, m, l = res[c * n_blk2 + nb]
                rows = slice(nb * BAND, (nb + 1) * BAND)
                a2_s[r, rows, :] = acc
                m2_s[r, rows, :] = bcast(m)
                l2_s[r, rows, :] = l
        return carry

    lax.fori_loop(0, N_RES // unroll2, g2_body, 0)

    n_blk1 = steps // chunk

    def g1_body(r4, carry):
        def rows_of(ref, a):
            return [ref[r4 + d4 * k, a * chunk:(a + 1) * chunk, :] for k in range(n_sub)]

        blocks = []
        for a in range(n_blk1):
            q = jnp.concatenate(rows_of(q1_ref, a), axis=0)
            if a == 0:
                blocks.append((q, jnp.concatenate(rows_of(kp_ref, 0), axis=0),
                               jnp.concatenate(rows_of(vp_ref, 0), axis=0), mask4_first))
            else:
                blocks.append((q,
                               jnp.concatenate(rows_of(kp_ref, a - 1) + rows_of(kp_ref, a), axis=0),
                               jnp.concatenate(rows_of(vp_ref, a - 1) + rows_of(vp_ref, a), axis=0),
                               mask4_band))
        for a, (acc, m, l) in enumerate(_band_blocks(blocks)):
            mb = bcast(m)
            for k in range(n_sub):
                dst = (r4 + d4 * k, slice(a * chunk, (a + 1) * chunk), slice(None))
                src = slice(k * chunk, (k + 1) * chunk)
                a1_s[dst] = acc[src, :]
                m1_s[dst] = mb[src, :]
                l1_s[dst] = l[src, :]
        return carry

    lax.fori_loop(0, d4, g1_body, 0)

    def merge_body(r, carry):
        a0 = a0_s[:, r].reshape(steps, LANES)
        m0 = m0_s[:, r].reshape(steps, LANES)
        l0 = l0_s[:, r].reshape(steps, LANES)
        a1, m1, l1 = a1_s[r], m1_s[r], l1_s[r]
        a2, m2, l2 = a2_s[r], m2_s[r], l2_s[r]
        mx = jnp.maximum(jnp.maximum(m0, m1), m2)
        w0, w1, w2 = jnp.exp(m0 - mx), jnp.exp(m1 - mx), jnp.exp(m2 - mx)
        num = w0 * a0 + w1 * a1 + w2 * a2
        den = w0 * l0 + w1 * l1 + w2 * l2
        gt = gate_ref[r].astype(F32)
        out_ref[r] = ((num * gt) / (den * (1.0 + jnp.exp(-gt)))).astype(out_ref.dtype)
        return carry

    lax.fori_loop(0, N_RES, merge_body, 0)


def _attn_call(q0, k_nat, v_nat, q12, gate, k_res, v_res, *, n_heads, name):
    b, seq, e = q0.shape
    steps = seq // N_RES
    nat = lambda off: pl.BlockSpec((None, seq, HEAD_DIM), lambda bi, h: (bi, 0, h + off))
    res = lambda off: pl.BlockSpec((None, N_RES, steps, HEAD_DIM), lambda bi, h: (bi, 0, 0, h + off))
    res_scratch = pltpu.VMEM((N_RES, steps, LANES), F32)
    blk_scratch = pltpu.VMEM((seq // BLK_ROWS, N_RES, SUBLANES, LANES), F32)
    return pl.pallas_call(
        _attn_kernel,
        grid=(b, n_heads),
        in_specs=[nat(0), nat(0), nat(0), res(0), res(n_heads), res(0), res(0), res(0)],
        out_specs=res(0),
        out_shape=jax.ShapeDtypeStruct((b, N_RES, steps, e), BF16),
        scratch_shapes=[blk_scratch] * 3 + [res_scratch] * 6,
        compiler_params=_cparams(2),
        name=name,
    )(q0, k_nat, v_nat, q12, q12, gate, k_res, v_res)


def _rope_tables(seq):
    inv_freq = 1.0 / (ROPE_THETA ** (jnp.arange(0, HEAD_DIM, 2, dtype=F32) / HEAD_DIM))
    ang = jnp.arange(seq, dtype=F32)[:, None] * inv_freq[None, :]
    cos, sin = jnp.cos(ang), jnp.sin(ang)
    return jnp.concatenate([cos, cos], axis=-1), jnp.concatenate([-sin, sin], axis=-1)


def _to_res_order(table, seq):
    return table.reshape(seq // N_RES, N_RES, -1).transpose(1, 0, 2).reshape(seq, -1)


def _to_blk_order(table, seq):
    steps = BLK_ROWS // N_RES
    return table.reshape(seq // BLK_ROWS, steps, N_RES, -1).transpose(0, 2, 1, 3).reshape(seq, -1)


def _perm_matrix(n_rows):
    steps = n_rows // N_RES
    idx = jnp.arange(n_rows)
    src = (idx % steps) * N_RES + idx // steps
    return (src[:, None] == idx[None, :]).astype(BF16)


def kernel(x, norm_a, w_in_a, w_grp_a, scale_a, w_out_a, norm_kv, w_k, w_v, norm_b, w_in_b,
           w_out_b, norm_f):
    b, seq, d = x.shape
    n = b * seq
    n_a = w_in_a.shape[0]
    n_b = w_in_b.shape[0]
    e_b = w_k.shape[1]
    n_heads = e_b // HEAD_DIM
    assert seq % (N_RES * BAND) == 0 and d % HEAD_DIM == 0

    rope_nat = _rope_tables(seq)
    rope_res = tuple(_to_res_order(t, seq) for t in rope_nat)
    rope_blk = tuple(_to_blk_order(t, seq) for t in rope_nat)
    pres, pblk = _perm_matrix(RES_ROWS), _perm_matrix(BLK_ROWS)

    xf = x.reshape(n, d)
    hdn = _rmsnorm_call(xf, norm_a[0])

    w_grp_bf, w_out_a_bf, w_out_b_bf = (w.astype(BF16) for w in (w_grp_a, w_out_a, w_out_b))
    for i in range(n_a):
        e_a = w_in_a.shape[2] // 2
        proj = _proj_call(hdn, w_in_a, layer=i, seq=seq, col_off=0, n_cols=2 * e_a, name=f"a{i}_in")
        if i < n_a - 1:
            gains = norm_a[i + 1][None]
            specs = [("nat", BF16)]
        else:
            gains = jnp.stack([norm_kv, norm_b[0], norm_b[0]])
            specs = [("nat", BF16), ("blk", BF16), ("res", BF16)]
        outs = _amix_call(proj, xf, w_grp_bf, scale_a, w_out_a_bf, gains, pres, pblk, layer=i,
                          seq=seq, norm_specs=specs, name=f"a{i}_mix")
        xf = outs[0]
        if i < n_a - 1:
            hdn = outs[1]
        else:
            hdn_kv, hdn_blk, hdn_res = outs[1], outs[2], outs[3]

    k_nat, k_res = _proj_call(hdn_kv, w_k[None], seq=seq, col_off=0, n_cols=e_b,
                              tables=rope_nat, perm_mat=pres, name="k")
    v_nat, v_res = _proj_call(hdn_kv, w_v[None], seq=seq, col_off=0, n_cols=e_b,
                              perm_mat=pres, name="v")
    k_nat, v_nat = k_nat.reshape(b, seq, e_b), v_nat.reshape(b, seq, e_b)

    out = None
    res_shape = lambda c: (b, N_RES, seq // N_RES, c)
    for i in range(n_b):
        hdn_res2 = hdn_res.reshape(n, d)
        q0 = _proj_call(hdn_blk, w_in_b, layer=i, seq=seq, col_off=0, n_cols=e_b, tables=rope_blk,
                        name=f"b{i}_q0")
        q12 = _proj_call(hdn_res2, w_in_b, layer=i, seq=seq, col_off=e_b, n_cols=2 * e_b,
                         tables=rope_res, name=f"b{i}_q12")
        gate = _proj_call(hdn_res2, w_in_b, layer=i, seq=seq, col_off=3 * e_b, n_cols=e_b,
                          name=f"b{i}_gate")
        hg = _attn_call(q0.reshape(b, seq, e_b), k_nat, v_nat, q12.reshape(res_shape(2 * e_b)),
                        gate.reshape(res_shape(e_b)), k_res, v_res, n_heads=n_heads,
                        name=f"b{i}_attn")
        if i < n_b - 1:
            gains = jnp.stack([norm_b[i + 1], norm_b[i + 1]])
            xf, hdn_blk, hdn_res = _bout_call(
                hg, xf, w_out_b_bf, gains, pres, pblk, layer=i, seq=seq, emit_x=True,
                norm_specs=[("blk", BF16), ("res", BF16)], name=f"b{i}_out")
        else:
            (out,) = _bout_call(hg, xf, w_out_b_bf, norm_f[None], pres, pblk, layer=i, seq=seq,
                                emit_x=False, norm_specs=[("nat", F32)], name=f"b{i}_out")
    return out.reshape(b, seq, d)
```

```python
import functools
import math

import jax
import jax.numpy as jnp
from jax import lax
from jax.experimental import pallas as pl
from jax.experimental.pallas import tpu as pltpu

F32 = jnp.float32
BF16 = jnp.bfloat16

RMS_EPS = 1e-6
POOL_WINDOWS = (2, 4, 8, 16)
POOL_HALO = 16
HEAD_DIM = 128
ROPE_THETA = 10000.0
NEG_INF = -1e30
N_RES = 16
DILATIONS = (1, 4, 16)
BAND = 128
SUBLANES = 8
LANES = 128
RES_ROWS = N_RES * N_RES
SUB_ROWS = RES_ROWS
BLK_ROWS = BAND
VMEM_LIMIT = 56 * 1024 * 1024
QK_SCALE = math.log2(math.e) / math.sqrt(HEAD_DIM)


def _cparams(n_axes):
    return pltpu.CompilerParams(
        dimension_semantics=("arbitrary",) * n_axes, vmem_limit_bytes=VMEM_LIMIT)


def _const_spec(shape, n_grid, layer=None):
    if layer is None:
        block, index = shape, (0,) * len(shape)
    else:
        block, index = (None,) + tuple(shape[1:]), (layer,) + (0,) * (len(shape) - 1)
    if n_grid == 1:
        return pl.BlockSpec(block, lambda i: index, pipeline_mode=pl.Buffered(1))
    return pl.BlockSpec(block, lambda j, i: index, pipeline_mode=pl.Buffered(1))


def _store_res(dst_ref, p_ref, rows, row0=0):
    steps = RES_ROWS // N_RES
    for sub in range(rows.shape[0] // RES_ROWS):
        pb = jnp.dot(p_ref[...], rows[sub * RES_ROWS:(sub + 1) * RES_ROWS, :],
                     preferred_element_type=F32).astype(dst_ref.dtype)
        s0 = row0 // N_RES + sub * steps
        for r in range(N_RES):
            dst_ref[r, s0:s0 + steps, :] = pb[r * steps:(r + 1) * steps, :]


def _rmsnorm_kernel(x_ref, g_ref, o_ref):
    xf = x_ref[...]
    inv = lax.rsqrt(jnp.mean(xf * xf, axis=-1, keepdims=True) + RMS_EPS)
    o_ref[...] = ((xf * inv) * g_ref[...]).astype(o_ref.dtype)


def _rmsnorm_call(x, g, bm=512):
    n, d = x.shape
    return pl.pallas_call(
        _rmsnorm_kernel,
        grid=(n // bm,),
        in_specs=[pl.BlockSpec((bm, d), lambda i: (i, 0)),
                  pl.BlockSpec((1, d), lambda i: (0, 0))],
        out_specs=pl.BlockSpec((bm, d), lambda i: (i, 0)),
        out_shape=jax.ShapeDtypeStruct((n, d), BF16),
        compiler_params=_cparams(1),
        name="rmsnorm_first",
    )(x, g.reshape(1, d))


def _emit_norms(xn, gains_ref, pres_ref, pblk_ref, out_refs, norm_specs, row0):
    inv = lax.rsqrt(jnp.mean(xn * xn, axis=-1, keepdims=True) + RMS_EPS)
    xh = xn * inv
    for k, (order, dtype) in enumerate(norm_specs):
        hd = (xh * gains_ref[k:k + 1, :]).astype(dtype)
        if order == "nat":
            out_refs[k][row0:row0 + xn.shape[0], :] = hd
        elif order == "res":
            _store_res(out_refs[k], pres_ref, hd, row0)
        else:
            for sub in range(xn.shape[0] // BLK_ROWS):
                src = slice(sub * BLK_ROWS, (sub + 1) * BLK_ROWS)
                dst = slice(row0 + sub * BLK_ROWS, row0 + (sub + 1) * BLK_ROWS)
                out_refs[k][dst, :] = jnp.dot(pblk_ref[...], hd[src, :],
                                              preferred_element_type=F32).astype(dtype)


def _norm_out(n, d, seq, bm, order, dtype):
    blocks_per_seq = seq // bm
    if order == "res":
        shape = jax.ShapeDtypeStruct((n // seq, N_RES, seq // N_RES, d), dtype)
        spec = pl.BlockSpec((None, N_RES, bm // N_RES, d),
                            lambda i: (i // blocks_per_seq, 0, i % blocks_per_seq, 0))
    else:
        shape = jax.ShapeDtypeStruct((n, d), dtype)
        spec = pl.BlockSpec((bm, d), lambda i: (i, 0))
    return shape, spec


def _proj_kernel(*refs, bm, bn, rope, perm_out):
    refs = list(refs)
    lhs_ref, w_ref = refs[:2]
    del refs[:2]
    if rope:
        cos_ref, sin_ref = refs[:2]
        del refs[:2]
    if perm_out:
        p_ref = refs.pop(0)
    o_ref = refs.pop(0)
    wbf_ref = refs.pop()

    @pl.when(pl.program_id(1) == 0)
    def _():
        wbf_ref[...] = w_ref[...].astype(BF16)

    acc = jnp.dot(lhs_ref[...], wbf_ref[...], preferred_element_type=F32)
    if rope:
        cos = cos_ref[...]
        sin = sin_ref[...]
        for hh in range(bn // HEAD_DIM):
            t = acc[:, hh * HEAD_DIM:(hh + 1) * HEAD_DIM]
            rot = pltpu.roll(t, HEAD_DIM // 2, 1)
            o_ref[:, hh * HEAD_DIM:(hh + 1) * HEAD_DIM] = (t * cos + rot * sin).astype(o_ref.dtype)
    else:
        o_ref[...] = acc.astype(o_ref.dtype)
    if perm_out:
        _store_res(refs[0], p_ref, o_ref[...])


def _proj_call(lhs, w, *, seq, col_off, n_cols, name, layer=0, tables=None, perm_mat=None,
               bm=512, bn=1024):
    n, k = lhs.shape
    blocks_per_seq = seq // bm
    col_blk0 = col_off // bn
    rope, perm_out = tables is not None, perm_mat is not None
    args = [lhs, w]
    in_specs = [pl.BlockSpec((bm, k), lambda j, i: (i, 0)),
                pl.BlockSpec((None, k, bn), lambda j, i: (layer, 0, j + col_blk0))]
    if rope:
        args += list(tables)
        in_specs += [pl.BlockSpec((bm, HEAD_DIM), lambda j, i: (i % blocks_per_seq, 0))] * 2
    if perm_out:
        args.append(perm_mat)
        in_specs.append(_const_spec(perm_mat.shape, 2))
    out_shape = [jax.ShapeDtypeStruct((n, n_cols), BF16)]
    out_specs = [pl.BlockSpec((bm, bn), lambda j, i: (i, j))]
    if perm_out:
        out_shape.append(jax.ShapeDtypeStruct((n // seq, N_RES, seq // N_RES, n_cols), BF16))
        out_specs.append(pl.BlockSpec(
            (None, N_RES, bm // N_RES, bn),
            lambda j, i: (i // blocks_per_seq, 0, i % blocks_per_seq, j)))
    res = pl.pallas_call(
        functools.partial(_proj_kernel, bm=bm, bn=bn, rope=rope, perm_out=perm_out),
        grid=(n_cols // bn, n // bm),
        in_specs=in_specs,
        out_specs=out_specs,
        out_shape=out_shape,
        scratch_shapes=[pltpu.VMEM((k, bn), BF16)],
        compiler_params=_cparams(2),
        name=name,
    )(*args)
    return res if perm_out else res[0]


def _amix_kernel(u_ref, g_ref, halo_ref, x_ref, wgrp_ref, scale_ref, wout_ref, gains_ref,
                 pres_ref, pblk_ref, xo_ref, *rest, bm, blocks_per_seq, norm_specs):
    out_refs = rest[:len(norm_specs)]
    (ext_ref,) = rest[len(norm_specs):]
    blk = pl.program_id(0) % blocks_per_seq
    gc = u_ref.shape[1] // len(POOL_WINDOWS)

    halo = halo_ref[...].astype(F32)
    ext_ref[0:POOL_HALO, :] = jnp.where(blk == 0, 0.0, halo)
    ext_ref[POOL_HALO:, :] = u_ref[...].astype(F32)

    items = [(row0, g) for row0 in range(0, bm, SUB_ROWS) for g in range(len(POOL_WINDOWS))]

    def pool(row0, g):
        w = POOL_WINDOWS[g]
        ext = ext_ref[row0:row0 + POOL_HALO + SUB_ROWS, g * gc:(g + 1) * gc]
        wsum, have = ext, 1
        while have < w:
            wsum = wsum + pltpu.roll(wsum, have, 0)
            have *= 2
        pos = blk * bm + row0 + lax.broadcasted_iota(jnp.int32, (SUB_ROWS, 1), 0)
        inv_cnt = 1.0 / jnp.minimum(pos + 1, w).astype(F32)
        return (wsum[POOL_HALO:, :] * inv_cnt - ext[POOL_HALO:, :]).astype(BF16)

    def gated(row0, g, pooled):
        cols = slice(g * gc, (g + 1) * gc)
        y = jnp.dot(pooled, wgrp_ref[g], preferred_element_type=F32) * scale_ref[:, cols]
        gt = g_ref[row0:row0 + SUB_ROWS, cols].astype(F32)
        return (y * (gt / (1.0 + jnp.exp(-gt)))).astype(BF16)

    def project(g, h):
        return jnp.dot(h, wout_ref[g * gc:(g + 1) * gc, :], preferred_element_type=F32)

    pooled, hs, upd = {}, {}, {}
    for t in range(len(items) + 2):
        if t < len(items):
            pooled[t] = pool(*items[t])
        if 0 <= t - 1 < len(items):
            hs[t - 1] = gated(*items[t - 1], pooled.pop(t - 1))
        if 0 <= t - 2 < len(items):
            row0, g = items[t - 2]
            part = project(g, hs.pop(t - 2))
            upd[row0] = part if g == 0 else upd[row0] + part
            if g == len(POOL_WINDOWS) - 1:
                xn = x_ref[row0:row0 + SUB_ROWS, :] + upd.pop(row0)
                xo_ref[row0:row0 + SUB_ROWS, :] = xn
                _emit_norms(xn, gains_ref, pres_ref, pblk_ref, out_refs, norm_specs, row0)


def _amix_call(proj, x, w_grp, scale, w_out, gains, pres, pblk, *, layer, seq, norm_specs, name,
               bm=512):
    n, d = x.shape
    e = proj.shape[1] // 2
    blocks_per_seq = seq // bm
    halo_per_blk = bm // POOL_HALO
    norm_shapes, norm_out_specs = zip(*[_norm_out(n, d, seq, bm, o, dt) for o, dt in norm_specs])
    kern = functools.partial(_amix_kernel, bm=bm, blocks_per_seq=blocks_per_seq,
                             norm_specs=tuple(norm_specs))
    return pl.pallas_call(
        kern,
        grid=(n // bm,),
        in_specs=[pl.BlockSpec((bm, e), lambda i: (i, 0)),
                  pl.BlockSpec((bm, e), lambda i: (i, 1)),
                  pl.BlockSpec((POOL_HALO, e), lambda i: (jnp.maximum(i * halo_per_blk - 1, 0), 0)),
                  pl.BlockSpec((bm, d), lambda i: (i, 0)),
                  _const_spec(w_grp.shape, 1, layer),
                  _const_spec((scale.shape[0], 1, e), 1, layer),
                  _const_spec(w_out.shape, 1, layer),
                  _const_spec(gains.shape, 1),
                  _const_spec(pres.shape, 1),
                  _const_spec(pblk.shape, 1)],
        out_specs=[pl.BlockSpec((bm, d), lambda i: (i, 0))] + list(norm_out_specs),
        out_shape=[jax.ShapeDtypeStruct((n, d), F32)] + list(norm_shapes),
        scratch_shapes=[pltpu.VMEM((bm + POOL_HALO, e), F32)],
        compiler_params=_cparams(1),
        name=name,
    )(proj, proj, proj, x, w_grp, scale.reshape(scale.shape[0], 1, e), w_out, gains, pres, pblk)


def _bout_kernel(hg_ref, x_ref, wout_ref, gains_ref, pres_ref, pblk_ref, *rest,
                 bm, emit_x, norm_specs):
    outs = rest
    steps = RES_ROWS // N_RES
    n_sub = bm // RES_ROWS
    hs = []
    for sub in range(n_sub):
        blk = jnp.concatenate(
            [hg_ref[r, sub * steps:(sub + 1) * steps, :] for r in range(N_RES)], axis=0)
        hs.append(jnp.dot(pres_ref[...], blk, preferred_element_type=F32).astype(BF16))
    xns = []
    for sub in range(n_sub):
        rows = slice(sub * RES_ROWS, (sub + 1) * RES_ROWS)
        xns.append(x_ref[rows, :] + jnp.dot(hs[sub], wout_ref[...], preferred_element_type=F32))
    for sub in range(n_sub):
        rows = slice(sub * RES_ROWS, (sub + 1) * RES_ROWS)
        if emit_x:
            outs[0][rows, :] = xns[sub]
        _emit_norms(xns[sub], gains_ref, pres_ref, pblk_ref, outs[1:] if emit_x else outs,
                    norm_specs, sub * RES_ROWS)


def _bout_call(hg, x, w_out, gains, pres, pblk, *, layer, seq, emit_x, norm_specs, name, bm=512):
    n, d = x.shape
    e = hg.shape[-1]
    blocks_per_seq = seq // bm
    norm_shapes, norm_out_specs = zip(*[_norm_out(n, d, seq, bm, o, dt) for o, dt in norm_specs])
    out_specs, out_shape = list(norm_out_specs), list(norm_shapes)
    if emit_x:
        out_specs.insert(0, pl.BlockSpec((bm, d), lambda i: (i, 0)))
        out_shape.insert(0, jax.ShapeDtypeStruct((n, d), F32))
    kern = functools.partial(_bout_kernel, bm=bm, emit_x=emit_x, norm_specs=tuple(norm_specs))
    return pl.pallas_call(
        kern,
        grid=(n // bm,),
        in_specs=[pl.BlockSpec((None, N_RES, bm // N_RES, e),
                               lambda i: (i // blocks_per_seq, 0, i % blocks_per_seq, 0)),
                  pl.BlockSpec((bm, d), lambda i: (i, 0)),
                  _const_spec(w_out.shape, 1, layer),
                  _const_spec(gains.shape, 1),
                  _const_spec(pres.shape, 1),
                  _const_spec(pblk.shape, 1)],
        out_specs=out_specs,
        out_shape=out_shape,
        compiler_params=_cparams(1),
        name=name,
    )(hg, x, w_out, gains, pres, pblk)


def _band_blocks(blocks):
    nt = (((1,), (1,)), ((), ()))
    ss = [lax.dot_general(q, k, nt, preferred_element_type=F32) for q, k, _, _ in blocks]
    ss = [jnp.where(blk[3], s, NEG_INF) for blk, s in zip(blocks, ss)]
    ms = [jnp.max(s, axis=-1, keepdims=True) for s in ss]
    ps = [jnp.exp2(s - m).astype(BF16) for s, m in zip(ss, ms)]
    pvs = [jnp.dot(p, jnp.concatenate([blk[2], jnp.ones_like(blk[2])], axis=1),
                   preferred_element_type=F32) for blk, p in zip(blocks, ps)]
    return [(pv[:, :HEAD_DIM], m, pv[:, HEAD_DIM:]) for pv, m in zip(pvs, ms)]


def _band_masks(row_idx, col_idx, row_idx1, col_idx1):
    band = (col_idx >= row_idx) & (col_idx <= row_idx + BAND)
    return band, col_idx1 <= row_idx1


def _attn_kernel(q0_ref, kn_ref, vn_ref, q1_ref, q2_ref, gate_ref, kp_ref, vp_ref, out_ref,
                 a0_s, m0_s, l0_s):
    seq = q0_ref.shape[0]
    steps = seq // N_RES
    row = lax.broadcasted_iota(jnp.int32, (BAND, 2 * BAND), 0)
    col = lax.broadcasted_iota(jnp.int32, (BAND, 2 * BAND), 1)
    row1 = lax.broadcasted_iota(jnp.int32, (BAND, BAND), 0)
    col1 = lax.broadcasted_iota(jnp.int32, (BAND, BAND), 1)
    mask_band, mask_first = _band_masks(row, col, row1, col1)
    blk_pos = lambda p: N_RES * (p % SUBLANES) + p // SUBLANES
    mask0_band, mask0_first = _band_masks(blk_pos(row), col, blk_pos(row1), col1)
    d4 = DILATIONS[1]
    chunk = BAND // d4
    n_sub = N_RES // d4
    d4_pos = lambda p: d4 * (p % chunk) + p // chunk
    mask4_band, mask4_first = _band_masks(
        d4_pos(row), d4_pos(col % BAND) + BAND * (col // BAND), d4_pos(row1), d4_pos(col1))

    def bcast(m):
        return jnp.broadcast_to(m, (m.shape[0], LANES))

    unroll0 = 16
    span = unroll0 * BAND
    tile = (N_RES, SUBLANES, LANES)

    def g0_group(it, first):
        base = 0 if first else pl.multiple_of(it * span, span)
        q_all = q0_ref[pl.ds(base, span), :]
        if first:
            k_all, v_all = kn_ref[0:span, :], vn_ref[0:span, :]
        else:
            lo = pl.multiple_of(base - BAND, BAND)
            k_all, v_all = kn_ref[pl.ds(lo, span + BAND), :], vn_ref[pl.ds(lo, span + BAND), :]
        blocks = []
        for u in range(unroll0):
            q = q_all[u * BAND:(u + 1) * BAND, :]
            if first and u == 0:
                blocks.append((q, k_all[0:BAND, :], v_all[0:BAND, :], mask0_first))
            else:
                k0 = (u - 1) * BAND if first else u * BAND
                blocks.append((q, k_all[k0:k0 + 2 * BAND, :], v_all[k0:k0 + 2 * BAND, :], mask0_band))
        for u, (acc, m, l) in enumerate(_band_blocks(blocks)):
            a0_s[it * unroll0 + u] = acc.reshape(tile)
            m0_s[it * unroll0 + u] = bcast(m).reshape(tile)
            l0_s[it * unroll0 + u] = l.reshape(tile)

    g0_group(0, True)

    def g0_body(it, carry):
        g0_group(it, False)
        return carry

    lax.fori_loop(1, seq // span, g0_body, 0)

    n_blk2 = steps // BAND
    n_blk1 = steps // chunk

    def class_body(r4, carry):
        def rows_of(ref, a):
            return [ref[r4 + d4 * k, a * chunk:(a + 1) * chunk, :] for k in range(n_sub)]

        blocks = []
        for k in range(n_sub):
            r = r4 + d4 * k
            q_all, k_all, v_all = q2_ref[r], kp_ref[r], vp_ref[r]
            for nb in range(n_blk2):
                q = q_all[nb * BAND:(nb + 1) * BAND, :]
                if nb == 0:
                    blocks.append((q, k_all[0:BAND, :], v_all[0:BAND, :], mask_first))
                else:
                    k0 = (nb - 1) * BAND
                    blocks.append((q, k_all[k0:k0 + 2 * BAND, :], v_all[k0:k0 + 2 * BAND, :], mask_band))
        for a in range(n_blk1):
            q = jnp.concatenate(rows_of(q1_ref, a), axis=0)
            if a == 0:
                blocks.append((q, jnp.concatenate(rows_of(kp_ref, 0), axis=0),
                               jnp.concatenate(rows_of(vp_ref, 0), axis=0), mask4_first))
            else:
                blocks.append((q,
                               jnp.concatenate(rows_of(kp_ref, a - 1) + rows_of(kp_ref, a), axis=0),
                               jnp.concatenate(rows_of(vp_ref, a - 1) + rows_of(vp_ref, a), axis=0),
                               mask4_band))
        res = _band_blocks(blocks)
        res2, res1 = res[:n_sub * n_blk2], res[n_sub * n_blk2:]

        for k in range(n_sub):
            r = r4 + d4 * k
            rows1 = slice(k * chunk, (k + 1) * chunk)
            for nb in range(n_blk2):
                rows = slice(nb * BAND, (nb + 1) * BAND)
                blk0 = slice(nb * BAND // SUBLANES, (nb + 1) * BAND // SUBLANES)
                a0 = a0_s[blk0, r].reshape(BAND, LANES)
                m0 = m0_s[blk0, r].reshape(BAND, LANES)
                l0 = l0_s[blk0, r].reshape(BAND, LANES)
                part1 = res1[nb * BAND // chunk:(nb + 1) * BAND // chunk]
                a1 = jnp.concatenate([acc[rows1, :] for acc, _, _ in part1], axis=0)
                m1 = jnp.concatenate([bcast(m[rows1, :]) for _, m, _ in part1], axis=0)
                l1 = jnp.concatenate([l[rows1, :] for _, _, l in part1], axis=0)
                a2, m2, l2 = res2[k * n_blk2 + nb]
                m2 = bcast(m2)
                mx = jnp.maximum(jnp.maximum(m0, m1), m2)
                w0, w1, w2 = jnp.exp2(m0 - mx), jnp.exp2(m1 - mx), jnp.exp2(m2 - mx)
                num = w0 * a0 + w1 * a1 + w2 * a2
                den = w0 * l0 + w1 * l1 + w2 * l2
                gt = gate_ref[r, rows, :].astype(F32)
                out_ref[r, rows, :] = ((num * gt) / (den * (1.0 + jnp.exp(-gt)))).astype(out_ref.dtype)
        return carry

    lax.fori_loop(0, d4, class_body, 0)


def _attn_call(q0, k_nat, v_nat, q12, gate, k_res, v_res, *, n_heads, name):
    b, seq, e = q0.shape
    steps = seq // N_RES
    nat = lambda off: pl.BlockSpec((None, seq, HEAD_DIM), lambda bi, h: (bi, 0, h + off))
    res = lambda off: pl.BlockSpec((None, N_RES, steps, HEAD_DIM), lambda bi, h: (bi, 0, 0, h + off))
    blk_scratch = pltpu.VMEM((seq // BLK_ROWS, N_RES, SUBLANES, LANES), F32)
    return pl.pallas_call(
        _attn_kernel,
        grid=(b, n_heads),
        in_specs=[nat(0), nat(0), nat(0), res(0), res(n_heads), res(0), res(0), res(0)],
        out_specs=res(0),
        out_shape=jax.ShapeDtypeStruct((b, N_RES, steps, e), BF16),
        scratch_shapes=[blk_scratch] * 3,
        compiler_params=_cparams(2),
        name=name,
    )(q0, k_nat, v_nat, q12, q12, gate, k_res, v_res)


def _rope_tables(seq):
    inv_freq = 1.0 / (ROPE_THETA ** (jnp.arange(0, HEAD_DIM, 2, dtype=F32) / HEAD_DIM))
    ang = jnp.arange(seq, dtype=F32)[:, None] * inv_freq[None, :]
    cos, sin = jnp.cos(ang), jnp.sin(ang)
    return jnp.concatenate([cos, cos], axis=-1), jnp.concatenate([-sin, sin], axis=-1)


def _to_res_order(table, seq):
    return table.reshape(seq // N_RES, N_RES, -1).transpose(1, 0, 2).reshape(seq, -1)


def _to_blk_order(table, seq):
    steps = BLK_ROWS // N_RES
    return table.reshape(seq // BLK_ROWS, steps, N_RES, -1).transpose(0, 2, 1, 3).reshape(seq, -1)


def _perm_matrix(n_rows):
    steps = n_rows // N_RES
    idx = jnp.arange(n_rows)
    src = (idx % steps) * N_RES + idx // steps
    return (src[:, None] == idx[None, :]).astype(BF16)


def kernel(x, norm_a, w_in_a, w_grp_a, scale_a, w_out_a, norm_kv, w_k, w_v, norm_b, w_in_b,
           w_out_b, norm_f):
    b, seq, d = x.shape
    n = b * seq
    n_a = w_in_a.shape[0]
    n_b = w_in_b.shape[0]
    e_b = w_k.shape[1]
    n_heads = e_b // HEAD_DIM
    assert seq % (N_RES * BAND) == 0 and d % HEAD_DIM == 0

    rope_nat = _rope_tables(seq)
    rope_res = tuple(_to_res_order(t, seq) * QK_SCALE for t in rope_nat)
    rope_blk = tuple(_to_blk_order(t, seq) * QK_SCALE for t in rope_nat)
    pres, pblk = _perm_matrix(RES_ROWS), _perm_matrix(BLK_ROWS)

    xf = x.reshape(n, d)
    hdn = _rmsnorm_call(xf, norm_a[0])

    w_grp_bf, w_out_a_bf, w_out_b_bf = (w.astype(BF16) for w in (w_grp_a, w_out_a, w_out_b))
    for i in range(n_a):
        e_a = w_in_a.shape[2] // 2
        proj = _proj_call(hdn, w_in_a, layer=i, seq=seq, col_off=0, n_cols=2 * e_a, name=f"a{i}_in")
        if i < n_a - 1:
            gains = norm_a[i + 1][None]
            specs = [("nat", BF16)]
        else:
            gains = jnp.stack([norm_kv, norm_b[0], norm_b[0]])
            specs = [("nat", BF16), ("blk", BF16), ("res", BF16)]
        outs = _amix_call(proj, xf, w_grp_bf, scale_a, w_out_a_bf, gains, pres, pblk, layer=i,
                          seq=seq, norm_specs=specs, name=f"a{i}_mix")
        xf = outs[0]
        if i < n_a - 1:
            hdn = outs[1]
        else:
            hdn_kv, hdn_blk, hdn_res = outs[1], outs[2], outs[3]

    k_nat, k_res = _proj_call(hdn_kv, w_k[None], seq=seq, col_off=0, n_cols=e_b,
                              tables=rope_nat, perm_mat=pres, name="k")
    v_nat, v_res = _proj_call(hdn_kv, w_v[None], seq=seq, col_off=0, n_cols=e_b,
                              perm_mat=pres, name="v")
    k_nat, v_nat = k_nat.reshape(b, seq, e_b), v_nat.reshape(b, seq, e_b)

    out = None
    res_shape = lambda c: (b, N_RES, seq // N_RES, c)
    for i in range(n_b):
        hdn_res2 = hdn_res.reshape(n, d)
        q0 = _proj_call(hdn_blk, w_in_b, layer=i, seq=seq, col_off=0, n_cols=e_b, tables=rope_blk,
                        name=f"b{i}_q0")
        q12 = _proj_call(hdn_res2, w_in_b, layer=i, seq=seq, col_off=e_b, n_cols=2 * e_b,
                         tables=rope_res, name=f"b{i}_q12")
        gate = _proj_call(hdn_res2, w_in_b, layer=i, seq=seq, col_off=3 * e_b, n_cols=e_b,
                          name=f"b{i}_gate")
        hg = _attn_call(q0.reshape(b, seq, e_b), k_nat, v_nat, q12.reshape(res_shape(2 * e_b)),
                        gate.reshape(res_shape(e_b)), k_res, v_res, n_heads=n_heads,
                        name=f"b{i}_attn")
        if i < n_b - 1:
            gains = jnp.stack([norm_b[i + 1], norm_b[i + 1]])
            xf, hdn_blk, hdn_res = _bout_call(
                hg, xf, w_out_b_bf, gains, pres, pblk, layer=i, seq=seq, emit_x=True,
                norm_specs=[("blk", BF16), ("res", BF16)], name=f"b{i}_out")
        else:
            (out,) = _bout_call(hg, xf, w_out_b_bf, norm_f[None], pres, pblk, layer=i, seq=seq,
                                emit_x=False, norm_specs=[("nat", F32)], name=f"b{i}_out")
    return out.reshape(b, seq, d)
```

```python
import functools
import math

import jax
import jax.numpy as jnp
from jax import lax
from jax.experimental import pallas as pl
from jax.experimental.pallas import tpu as pltpu

F32 = jnp.float32
BF16 = jnp.bfloat16

RMS_EPS = 1e-6
POOL_WINDOWS = (2, 4, 8, 16)
POOL_HALO = 16
HEAD_DIM = 128
ROPE_THETA = 10000.0
NEG_INF = -1e30
N_RES = 16
DILATIONS = (1, 4, 16)
BAND = 128
SUBLANES = 8
LANES = 128
RES_ROWS = N_RES * N_RES
SUB_ROWS = RES_ROWS
BLK_ROWS = BAND
VMEM_LIMIT = 56 * 1024 * 1024
QK_SCALE = math.log2(math.e) / math.sqrt(HEAD_DIM)


def _cparams(n_axes):
    return pltpu.CompilerParams(
        dimension_semantics=("arbitrary",) * n_axes, vmem_limit_bytes=VMEM_LIMIT)


def _const_spec(shape, n_grid, layer=None):
    if layer is None:
        block, index = shape, (0,) * len(shape)
    else:
        block, index = (None,) + tuple(shape[1:]), (layer,) + (0,) * (len(shape) - 1)
    if n_grid == 1:
        return pl.BlockSpec(block, lambda i: index, pipeline_mode=pl.Buffered(1))
    return pl.BlockSpec(block, lambda j, i: index, pipeline_mode=pl.Buffered(1))


def _store_res(dst_ref, p_ref, rows, row0=0):
    steps = RES_ROWS // N_RES
    for sub in range(rows.shape[0] // RES_ROWS):
        pb = jnp.dot(p_ref[...], rows[sub * RES_ROWS:(sub + 1) * RES_ROWS, :],
                     preferred_element_type=F32).astype(dst_ref.dtype)
        s0 = row0 // N_RES + sub * steps
        for r in range(N_RES):
            dst_ref[r, s0:s0 + steps, :] = pb[r * steps:(r + 1) * steps, :]


def _rmsnorm_kernel(x_ref, g_ref, o_ref):
    xf = x_ref[...]
    inv = lax.rsqrt(jnp.mean(xf * xf, axis=-1, keepdims=True) + RMS_EPS)
    o_ref[...] = ((xf * inv) * g_ref[...]).astype(o_ref.dtype)


def _rmsnorm_call(x, g, bm=512):
    n, d = x.shape
    return pl.pallas_call(
        _rmsnorm_kernel,
        grid=(n // bm,),
        in_specs=[pl.BlockSpec((bm, d), lambda i: (i, 0)),
                  pl.BlockSpec((1, d), lambda i: (0, 0))],
        out_specs=pl.BlockSpec((bm, d), lambda i: (i, 0)),
        out_shape=jax.ShapeDtypeStruct((n, d), BF16),
        compiler_params=_cparams(1),
        name="rmsnorm_first",
    )(x, g.reshape(1, d))


def _emit_norms(xn, gains_ref, pres_ref, pblk_ref, out_refs, norm_specs, row0):
    inv = lax.rsqrt(jnp.mean(xn * xn, axis=-1, keepdims=True) + RMS_EPS)
    xh = xn * inv
    for k, (order, dtype) in enumerate(norm_specs):
        hd = (xh * gains_ref[k:k + 1, :]).astype(dtype)
        if order == "nat":
            out_refs[k][row0:row0 + xn.shape[0], :] = hd
        elif order == "res":
            _store_res(out_refs[k], pres_ref, hd, row0)
        else:
            for sub in range(xn.shape[0] // BLK_ROWS):
                src = slice(sub * BLK_ROWS, (sub + 1) * BLK_ROWS)
                dst = slice(row0 + sub * BLK_ROWS, row0 + (sub + 1) * BLK_ROWS)
                out_refs[k][dst, :] = jnp.dot(pblk_ref[...], hd[src, :],
                                              preferred_element_type=F32).astype(dtype)


def _norm_out(n, d, seq, bm, order, dtype):
    blocks_per_seq = seq // bm
    if order == "res":
        shape = jax.ShapeDtypeStruct((n // seq, N_RES, seq // N_RES, d), dtype)
        spec = pl.BlockSpec((None, N_RES, bm // N_RES, d),
                            lambda i: (i // blocks_per_seq, 0, i % blocks_per_seq, 0))
    else:
        shape = jax.ShapeDtypeStruct((n, d), dtype)
        spec = pl.BlockSpec((bm, d), lambda i: (i, 0))
    return shape, spec


def _proj_kernel(*refs, bm, bn, rope, perm_out):
    refs = list(refs)
    lhs_ref, w_ref = refs[:2]
    del refs[:2]
    if rope:
        cos_ref, sin_ref = refs[:2]
        del refs[:2]
    if perm_out:
        p_ref = refs.pop(0)
    o_ref = refs.pop(0)
    wbf_ref = refs.pop()

    @pl.when(pl.program_id(1) == 0)
    def _():
        wbf_ref[...] = w_ref[...].astype(BF16)

    acc = jnp.dot(lhs_ref[...], wbf_ref[...], preferred_element_type=F32)
    if rope:
        cos = cos_ref[...]
        sin = sin_ref[...]
        for hh in range(bn // HEAD_DIM):
            t = acc[:, hh * HEAD_DIM:(hh + 1) * HEAD_DIM]
            rot = pltpu.roll(t, HEAD_DIM // 2, 1)
            o_ref[:, hh * HEAD_DIM:(hh + 1) * HEAD_DIM] = (t * cos + rot * sin).astype(o_ref.dtype)
    else:
        o_ref[...] = acc.astype(o_ref.dtype)
    if perm_out:
        _store_res(refs[0], p_ref, o_ref[...])


def _proj_call(lhs, w, *, seq, col_off, n_cols, name, layer=0, tables=None, perm_mat=None,
               bm=1024, bn=1024):
    n, k = lhs.shape
    blocks_per_seq = seq // bm
    col_blk0 = col_off // bn
    rope, perm_out = tables is not None, perm_mat is not None
    args = [lhs, w]
    in_specs = [pl.BlockSpec((bm, k), lambda j, i: (i, 0)),
                pl.BlockSpec((None, k, bn), lambda j, i: (layer, 0, j + col_blk0))]
    if rope:
        args += list(tables)
        in_specs += [pl.BlockSpec((bm, HEAD_DIM), lambda j, i: (i % blocks_per_seq, 0))] * 2
    if perm_out:
        args.append(perm_mat)
        in_specs.append(_const_spec(perm_mat.shape, 2))
    out_shape = [jax.ShapeDtypeStruct((n, n_cols), BF16)]
    out_specs = [pl.BlockSpec((bm, bn), lambda j, i: (i, j))]
    if perm_out:
        out_shape.append(jax.ShapeDtypeStruct((n // seq, N_RES, seq // N_RES, n_cols), BF16))
        out_specs.append(pl.BlockSpec(
            (None, N_RES, bm // N_RES, bn),
            lambda j, i: (i // blocks_per_seq, 0, i % blocks_per_seq, j)))
    res = pl.pallas_call(
        functools.partial(_proj_kernel, bm=bm, bn=bn, rope=rope, perm_out=perm_out),
        grid=(n_cols // bn, n // bm),
        in_specs=in_specs,
        out_specs=out_specs,
        out_shape=out_shape,
        scratch_shapes=[pltpu.VMEM((k, bn), BF16)],
        compiler_params=_cparams(2),
        name=name,
    )(*args)
    return res if perm_out else res[0]


def _wcomb_kernel(wu_ref, grp_ref, scale_ref, o_ref):
    grp = (grp_ref[...] * scale_ref[...]).astype(BF16)
    o_ref[...] = jnp.dot(wu_ref[...].astype(BF16), grp, preferred_element_type=F32).astype(o_ref.dtype)


def _wcomb_call(w_in, w_grp, scale, *, layer, name, bk=1024):
    n_layers, k, e2 = w_in.shape
    n_grp, gc = w_grp.shape[1], w_grp.shape[2]
    return pl.pallas_call(
        _wcomb_kernel,
        grid=(n_grp, k // bk),
        in_specs=[pl.BlockSpec((None, bk, gc), lambda g, i: (layer, i, g)),
                  pl.BlockSpec((None, None, gc, gc), lambda g, i: (layer, g, 0, 0)),
                  pl.BlockSpec((None, 1, gc), lambda g, i: (layer * n_grp + g, 0, 0))],
        out_specs=pl.BlockSpec((bk, gc), lambda g, i: (i, g)),
        out_shape=jax.ShapeDtypeStruct((k, e2 // 2), BF16),
        compiler_params=_cparams(2),
        name=name,
    )(w_in, w_grp, scale.reshape(n_layers * n_grp, 1, gc))


def _ah_kernel(hdn_ref, wc_ref, wg_ref, h_ref, wgbf_ref, carry_ref, *, bm, blocks_per_seq):
    i = pl.program_id(0)
    blk = i % blocks_per_seq
    gc = h_ref.shape[1] // len(POOL_WINDOWS)

    @pl.when(i == 0)
    def _():
        wgbf_ref[...] = wg_ref[...].astype(BF16)

    @pl.when(blk == 0)
    def _():
        carry_ref[...] = jnp.zeros_like(carry_ref)

    hdn = hdn_ref[...]
    pos = blk * bm + lax.broadcasted_iota(jnp.int32, (bm, 1), 0)

    def finish(g, acc_y, acc_g):
        w = POOL_WINDOWS[g]
        cols = slice(g * gc, (g + 1) * gc)
        ext = jnp.concatenate([carry_ref[:, cols], acc_y], axis=0)
        carry_ref[:, cols] = acc_y[bm - POOL_HALO:, :]
        wsum, have = ext, 1
        while have < w:
            wsum = wsum + pltpu.roll(wsum, have, 0)
            have *= 2
        inv_cnt = 1.0 / jnp.minimum(pos + 1, w).astype(F32)
        pooled = wsum[POOL_HALO:, :] * inv_cnt - acc_y
        h_ref[:, cols] = (pooled * (acc_g / (1.0 + jnp.exp(-acc_g)))).astype(h_ref.dtype)

    accs = {}
    for t in range(len(POOL_WINDOWS) + 1):
        if t < len(POOL_WINDOWS):
            cols = slice(t * gc, (t + 1) * gc)
            accs[t] = (jnp.dot(hdn, wc_ref[:, cols], preferred_element_type=F32),
                       jnp.dot(hdn, wgbf_ref[:, cols], preferred_element_type=F32))
        if t >= 1:
            finish(t - 1, *accs.pop(t - 1))


def _ah_call(hdn, w_comb, w_in, *, layer, seq, name, bm=512):
    n, k = hdn.shape
    e = w_comb.shape[1]
    kern = functools.partial(_ah_kernel, bm=bm, blocks_per_seq=seq // bm)
    return pl.pallas_call(
        kern,
        grid=(n // bm,),
        in_specs=[pl.BlockSpec((bm, k), lambda i: (i, 0)),
                  _const_spec(w_comb.shape, 1),
                  pl.BlockSpec((None, k, e), lambda i: (layer, 0, 1), pipeline_mode=pl.Buffered(1))],
        out_specs=pl.BlockSpec((bm, e), lambda i: (i, 0)),
        out_shape=jax.ShapeDtypeStruct((n, e), BF16),
        scratch_shapes=[pltpu.VMEM((k, e), BF16), pltpu.VMEM((POOL_HALO, e), F32)],
        compiler_params=_cparams(1),
        name=name,
    )(hdn, w_comb, w_in)


def _bout_kernel(hg_ref, x_ref, wout_ref, gains_ref, pres_ref, pblk_ref, *rest,
                 bm, res_in, emit_x, norm_specs):
    outs = rest
    steps = RES_ROWS // N_RES
    n_sub = bm // RES_ROWS
    hs = []
    for sub in range(n_sub):
        if not res_in:
            hs.append(hg_ref[sub * RES_ROWS:(sub + 1) * RES_ROWS, :])
            continue
        blk = jnp.concatenate(
            [hg_ref[r, sub * steps:(sub + 1) * steps, :] for r in range(N_RES)], axis=0)
        hs.append(jnp.dot(pres_ref[...], blk, preferred_element_type=F32).astype(BF16))
    xns = []
    for sub in range(n_sub):
        rows = slice(sub * RES_ROWS, (sub + 1) * RES_ROWS)
        xns.append(x_ref[rows, :] + jnp.dot(hs[sub], wout_ref[...], preferred_element_type=F32))
    for sub in range(n_sub):
        rows = slice(sub * RES_ROWS, (sub + 1) * RES_ROWS)
        if emit_x:
            outs[0][rows, :] = xns[sub]
        _emit_norms(xns[sub], gains_ref, pres_ref, pblk_ref, outs[1:] if emit_x else outs,
                    norm_specs, sub * RES_ROWS)


def _bout_call(hg, x, w_out, gains, pres, pblk, *, layer, seq, emit_x, norm_specs, name, bm=512):
    n, d = x.shape
    e = hg.shape[-1]
    blocks_per_seq = seq // bm
    res_in = hg.ndim == 4
    norm_shapes, norm_out_specs = zip(*[_norm_out(n, d, seq, bm, o, dt) for o, dt in norm_specs])
    out_specs, out_shape = list(norm_out_specs), list(norm_shapes)
    if emit_x:
        out_specs.insert(0, pl.BlockSpec((bm, d), lambda i: (i, 0)))
        out_shape.insert(0, jax.ShapeDtypeStruct((n, d), F32))
    kern = functools.partial(_bout_kernel, bm=bm, res_in=res_in, emit_x=emit_x,
                             norm_specs=tuple(norm_specs))
    if res_in:
        hg_spec = pl.BlockSpec((None, N_RES, bm // N_RES, e),
                               lambda i: (i // blocks_per_seq, 0, i % blocks_per_seq, 0))
    else:
        hg_spec = pl.BlockSpec((bm, e), lambda i: (i, 0))
    return pl.pallas_call(
        kern,
        grid=(n // bm,),
        in_specs=[hg_spec,
                  pl.BlockSpec((bm, d), lambda i: (i, 0)),
                  _const_spec(w_out.shape, 1, layer),
                  _const_spec(gains.shape, 1),
                  _const_spec(pres.shape, 1),
                  _const_spec(pblk.shape, 1)],
        out_specs=out_specs,
        out_shape=out_shape,
        compiler_params=_cparams(1),
        name=name,
    )(hg, x, w_out, gains, pres, pblk)


def _band_blocks(blocks):
    nt = (((1,), (1,)), ((), ()))
    ss = [lax.dot_general(q, k, nt, preferred_element_type=F32) for q, k, _, _ in blocks]
    ss = [jnp.where(blk[3], s, NEG_INF) for blk, s in zip(blocks, ss)]
    ms = [jnp.max(s, axis=-1, keepdims=True) for s in ss]
    ps = [jnp.exp2(s - m).astype(BF16) for s, m in zip(ss, ms)]
    pvs = [jnp.dot(p, jnp.concatenate([blk[2], jnp.ones_like(blk[2])], axis=1),
                   preferred_element_type=F32) for blk, p in zip(blocks, ps)]
    return [(pv[:, :HEAD_DIM], m, pv[:, HEAD_DIM:]) for pv, m in zip(pvs, ms)]


def _band_masks(row_idx, col_idx, row_idx1, col_idx1):
    band = (col_idx >= row_idx) & (col_idx <= row_idx + BAND)
    return band, col_idx1 <= row_idx1


def _attn_kernel(q0_ref, kn_ref, vn_ref, q1_ref, q2_ref, gate_ref, kp_ref, vp_ref, out_ref,
                 a0_s, m0_s, l0_s):
    seq = q0_ref.shape[0]
    steps = seq // N_RES
    row = lax.broadcasted_iota(jnp.int32, (BAND, 2 * BAND), 0)
    col = lax.broadcasted_iota(jnp.int32, (BAND, 2 * BAND), 1)
    row1 = lax.broadcasted_iota(jnp.int32, (BAND, BAND), 0)
    col1 = lax.broadcasted_iota(jnp.int32, (BAND, BAND), 1)
    mask_band, mask_first = _band_masks(row, col, row1, col1)
    blk_pos = lambda p: N_RES * (p % SUBLANES) + p // SUBLANES
    mask0_band, mask0_first = _band_masks(blk_pos(row), col, blk_pos(row1), col1)
    d4 = DILATIONS[1]
    chunk = BAND // d4
    n_sub = N_RES // d4
    d4_pos = lambda p: d4 * (p % chunk) + p // chunk
    mask4_band, mask4_first = _band_masks(
        d4_pos(row), d4_pos(col % BAND) + BAND * (col // BAND), d4_pos(row1), d4_pos(col1))

    def bcast(m):
        return jnp.broadcast_to(m, (m.shape[0], LANES))

    unroll0 = 16
    span = unroll0 * BAND
    tile = (N_RES, SUBLANES, LANES)

    def g0_group(it, first):
        base = 0 if first else pl.multiple_of(it * span, span)
        q_all = q0_ref[pl.ds(base, span), :]
        if first:
            k_all, v_all = kn_ref[0:span, :], vn_ref[0:span, :]
        else:
            lo = pl.multiple_of(base - BAND, BAND)
            k_all, v_all = kn_ref[pl.ds(lo, span + BAND), :], vn_ref[pl.ds(lo, span + BAND), :]
        blocks = []
        for u in range(unroll0):
            q = q_all[u * BAND:(u + 1) * BAND, :]
            if first and u == 0:
                blocks.append((q, k_all[0:BAND, :], v_all[0:BAND, :], mask0_first))
            else:
                k0 = (u - 1) * BAND if first else u * BAND
                blocks.append((q, k_all[k0:k0 + 2 * BAND, :], v_all[k0:k0 + 2 * BAND, :], mask0_band))
        for u, (acc, m, l) in enumerate(_band_blocks(blocks)):
            a0_s[it * unroll0 + u] = acc.reshape(tile)
            m0_s[it * unroll0 + u] = bcast(m).reshape(tile)
            l0_s[it * unroll0 + u] = l.reshape(tile)

    g0_group(0, True)

    def g0_body(it, carry):
        g0_group(it, False)
        return carry

    lax.fori_loop(1, seq // span, g0_body, 0)

    n_blk2 = steps // BAND
    n_blk1 = steps // chunk

    def class_body(r4, carry):
        def rows_of(ref, a):
            return [ref[r4 + d4 * k, a * chunk:(a + 1) * chunk, :] for k in range(n_sub)]

        blocks = []
        for k in range(n_sub):
            r = r4 + d4 * k
            q_all, k_all, v_all = q2_ref[r], kp_ref[r], vp_ref[r]
            for nb in range(n_blk2):
                q = q_all[nb * BAND:(nb + 1) * BAND, :]
                if nb == 0:
                    blocks.append((q, k_all[0:BAND, :], v_all[0:BAND, :], mask_first))
                else:
                    k0 = (nb - 1) * BAND
                    blocks.append((q, k_all[k0:k0 + 2 * BAND, :], v_all[k0:k0 + 2 * BAND, :], mask_band))
        for a in range(n_blk1):
            q = jnp.concatenate(rows_of(q1_ref, a), axis=0)
            if a == 0:
                blocks.append((q, jnp.concatenate(rows_of(kp_ref, 0), axis=0),
                               jnp.concatenate(rows_of(vp_ref, 0), axis=0), mask4_first))
            else:
                blocks.append((q,
                               jnp.concatenate(rows_of(kp_ref, a - 1) + rows_of(kp_ref, a), axis=0),
                               jnp.concatenate(rows_of(vp_ref, a - 1) + rows_of(vp_ref, a), axis=0),
                               mask4_band))
        res = _band_blocks(blocks)
        res2, res1 = res[:n_sub * n_blk2], res[n_sub * n_blk2:]

        for k in range(n_sub):
            r = r4 + d4 * k
            rows1 = slice(k * chunk, (k + 1) * chunk)
            for nb in range(n_blk2):
                rows = slice(nb * BAND, (nb + 1) * BAND)
                blk0 = slice(nb * BAND // SUBLANES, (nb + 1) * BAND // SUBLANES)
                a0 = a0_s[blk0, r].reshape(BAND, LANES)
                m0 = m0_s[blk0, r].reshape(BAND, LANES)
                l0 = l0_s[blk0, r].reshape(BAND, LANES)
                part1 = res1[nb * BAND // chunk:(nb + 1) * BAND // chunk]
                a1 = jnp.concatenate([acc[rows1, :] for acc, _, _ in part1], axis=0)
                m1 = jnp.concatenate([bcast(m[rows1, :]) for _, m, _ in part1], axis=0)
                l1 = jnp.concatenate([l[rows1, :] for _, _, l in part1], axis=0)
                a2, m2, l2 = res2[k * n_blk2 + nb]
                m2 = bcast(m2)
                mx = jnp.maximum(jnp.maximum(m0, m1), m2)
                w0, w1, w2 = jnp.exp2(m0 - mx), jnp.exp2(m1 - mx), jnp.exp2(m2 - mx)
                num = w0 * a0 + w1 * a1 + w2 * a2
                den = w0 * l0 + w1 * l1 + w2 * l2
                gt = gate_ref[r, rows, :].astype(F32)
                out_ref[r, rows, :] = ((num * gt) / (den * (1.0 + jnp.exp(-gt)))).astype(out_ref.dtype)
        return carry

    lax.fori_loop(0, d4, class_body, 0)


def _attn_call(q0, k_nat, v_nat, q12, gate, k_res, v_res, *, n_heads, name):
    b, seq, e = q0.shape
    steps = seq // N_RES
    nat = lambda off: pl.BlockSpec((None, seq, HEAD_DIM), lambda bi, h: (bi, 0, h + off))
    res = lambda off: pl.BlockSpec((None, N_RES, steps, HEAD_DIM), lambda bi, h: (bi, 0, 0, h + off))
    blk_scratch = pltpu.VMEM((seq // BLK_ROWS, N_RES, SUBLANES, LANES), F32)
    return pl.pallas_call(
        _attn_kernel,
        grid=(b, n_heads),
        in_specs=[nat(0), nat(0), nat(0), res(0), res(n_heads), res(0), res(0), res(0)],
        out_specs=res(0),
        out_shape=jax.ShapeDtypeStruct((b, N_RES, steps, e), BF16),
        scratch_shapes=[blk_scratch] * 3,
        compiler_params=_cparams(2),
        name=name,
    )(q0, k_nat, v_nat, q12, q12, gate, k_res, v_res)


def _rope_tables(seq):
    inv_freq = 1.0 / (ROPE_THETA ** (jnp.arange(0, HEAD_DIM, 2, dtype=F32) / HEAD_DIM))
    ang = jnp.arange(seq, dtype=F32)[:, None] * inv_freq[None, :]
    cos, sin = jnp.cos(ang), jnp.sin(ang)
    return jnp.concatenate([cos, cos], axis=-1), jnp.concatenate([-sin, sin], axis=-1)


def _to_res_order(table, seq):
    return table.reshape(seq // N_RES, N_RES, -1).transpose(1, 0, 2).reshape(seq, -1)


def _to_blk_order(table, seq):
    steps = BLK_ROWS // N_RES
    return table.reshape(seq // BLK_ROWS, steps, N_RES, -1).transpose(0, 2, 1, 3).reshape(seq, -1)


def _perm_matrix(n_rows):
    steps = n_rows // N_RES
    idx = jnp.arange(n_rows)
    src = (idx % steps) * N_RES + idx // steps
    return (src[:, None] == idx[None, :]).astype(BF16)


def kernel(x, norm_a, w_in_a, w_grp_a, scale_a, w_out_a, norm_kv, w_k, w_v, norm_b, w_in_b,
           w_out_b, norm_f):
    b, seq, d = x.shape
    n = b * seq
    n_a = w_in_a.shape[0]
    n_b = w_in_b.shape[0]
    e_b = w_k.shape[1]
    n_heads = e_b // HEAD_DIM
    assert seq % (N_RES * BAND) == 0 and d % HEAD_DIM == 0

    rope_nat = _rope_tables(seq)
    rope_res = tuple(_to_res_order(t, seq) * QK_SCALE for t in rope_nat)
    rope_blk = tuple(_to_blk_order(t, seq) * QK_SCALE for t in rope_nat)
    pres, pblk = _perm_matrix(RES_ROWS), _perm_matrix(BLK_ROWS)

    xf = x.reshape(n, d)
    hdn = _rmsnorm_call(xf, norm_a[0])

    w_out_a_bf, w_out_b_bf = w_out_a.astype(BF16), w_out_b.astype(BF16)
    for i in range(n_a):
        w_comb = _wcomb_call(w_in_a, w_grp_a, scale_a, layer=i, name=f"a{i}_wcomb")
        h = _ah_call(hdn, w_comb, w_in_a, layer=i, seq=seq, name=f"a{i}_h")
        if i < n_a - 1:
            gains = norm_a[i + 1][None]
            specs = [("nat", BF16)]
        else:
            gains = jnp.stack([norm_kv, norm_b[0], norm_b[0]])
            specs = [("nat", BF16), ("blk", BF16), ("res", BF16)]
        outs = _bout_call(h, xf, w_out_a_bf, gains, pres, pblk, layer=i, seq=seq, emit_x=True,
                          norm_specs=specs, name=f"a{i}_out")
        xf = outs[0]
        if i < n_a - 1:
            hdn = outs[1]
        else:
            hdn_kv, hdn_blk, hdn_res = outs[1], outs[2], outs[3]

    k_nat, k_res = _proj_call(hdn_kv, w_k[None], seq=seq, col_off=0, n_cols=e_b,
                              tables=rope_nat, perm_mat=pres, name="k")
    v_nat, v_res = _proj_call(hdn_kv, w_v[None], seq=seq, col_off=0, n_cols=e_b,
                              perm_mat=pres, name="v")
    k_nat, v_nat = k_nat.reshape(b, seq, e_b), v_nat.reshape(b, seq, e_b)

    out = None
    res_shape = lambda c: (b, N_RES, seq // N_RES, c)
    for i in range(n_b):
        hdn_res2 = hdn_res.reshape(n, d)
        q0 = _proj_call(hdn_blk, w_in_b, layer=i, seq=seq, col_off=0, n_cols=e_b, tables=rope_blk,
                        name=f"b{i}_q0")
        q12 = _proj_call(hdn_res2, w_in_b, layer=i, seq=seq, col_off=e_b, n_cols=2 * e_b,
                         tables=rope_res, name=f"b{i}_q12")
        gate = _proj_call(hdn_res2, w_in_b, layer=i, seq=seq, col_off=3 * e_b, n_cols=e_b,
                          name=f"b{i}_gate")
        hg = _attn_call(q0.reshape(b, seq, e_b), k_nat, v_nat, q12.reshape(res_shape(2 * e_b)),
                        gate.reshape(res_shape(e_b)), k_res, v_res, n_heads=n_heads,
                        name=f"b{i}_attn")
        if i < n_b - 1:
            gains = jnp.stack([norm_b[i + 1], norm_b[i + 1]])
            xf, hdn_blk, hdn_res = _bout_call(
                hg, xf, w_out_b_bf, gains, pres, pblk, layer=i, seq=seq, emit_x=True,
                norm_specs=[("blk", BF16), ("res", BF16)], name=f"b{i}_out")
        else:
            (out,) = _bout_call(hg, xf, w_out_b_bf, norm_f[None], pres, pblk, layer=i, seq=seq,
                                emit_x=False, norm_specs=[("nat", F32)], name=f"b{i}_out")
    return out.reshape(b, seq, d)
```

```python
import functools
import math

import jax
import jax.numpy as jnp
from jax import lax
from jax.experimental import pallas as pl
from jax.experimental.pallas import tpu as pltpu

F32 = jnp.float32
BF16 = jnp.bfloat16

RMS_EPS = 1e-6
POOL_WINDOWS = (2, 4, 8, 16)
POOL_HALO = 16
HEAD_DIM = 128
ROPE_THETA = 10000.0
NEG_INF = -1e30
N_RES = 16
DILATIONS = (1, 4, 16)
BAND = 128
SUBLANES = 8
LANES = 128
RES_ROWS = N_RES * N_RES
SUB_ROWS = RES_ROWS
BLK_ROWS = BAND
VMEM_LIMIT = 56 * 1024 * 1024
QK_SCALE = math.log2(math.e) / math.sqrt(HEAD_DIM)


def _cparams(n_axes):
    return pltpu.CompilerParams(
        dimension_semantics=("arbitrary",) * n_axes, vmem_limit_bytes=VMEM_LIMIT)


def _const_spec(shape, n_grid, layer=None):
    if layer is None:
        block, index = shape, (0,) * len(shape)
    else:
        block, index = (None,) + tuple(shape[1:]), (layer,) + (0,) * (len(shape) - 1)
    if n_grid == 1:
        return pl.BlockSpec(block, lambda i: index, pipeline_mode=pl.Buffered(1))
    return pl.BlockSpec(block, lambda j, i: index, pipeline_mode=pl.Buffered(1))


def _store_res(dst_ref, p_ref, rows, row0=0):
    steps = RES_ROWS // N_RES
    for sub in range(rows.shape[0] // RES_ROWS):
        pb = jnp.dot(p_ref[...], rows[sub * RES_ROWS:(sub + 1) * RES_ROWS, :],
                     preferred_element_type=F32).astype(dst_ref.dtype)
        s0 = row0 // N_RES + sub * steps
        for r in range(N_RES):
            dst_ref[r, s0:s0 + steps, :] = pb[r * steps:(r + 1) * steps, :]


def _emit_norms(xn, gains_ref, pres_ref, pblk_ref, out_refs, norm_specs, row0):
    inv = lax.rsqrt(jnp.mean(xn * xn, axis=-1, keepdims=True) + RMS_EPS)
    xh = xn * inv
    for k, (order, dtype) in enumerate(norm_specs):
        hd = (xh * gains_ref[k:k + 1, :]).astype(dtype)
        if order == "nat":
            out_refs[k][row0:row0 + xn.shape[0], :] = hd
        elif order == "res":
            _store_res(out_refs[k], pres_ref, hd, row0)
        else:
            for sub in range(xn.shape[0] // BLK_ROWS):
                src = slice(sub * BLK_ROWS, (sub + 1) * BLK_ROWS)
                dst = slice(row0 + sub * BLK_ROWS, row0 + (sub + 1) * BLK_ROWS)
                out_refs[k][dst, :] = jnp.dot(pblk_ref[...], hd[src, :],
                                              preferred_element_type=F32).astype(dtype)


def _norm_out(n, d, seq, bm, order, dtype):
    blocks_per_seq = seq // bm
    if order == "res":
        shape = jax.ShapeDtypeStruct((n // seq, N_RES, seq // N_RES, d), dtype)
        spec = pl.BlockSpec((None, N_RES, bm // N_RES, d),
                            lambda i: (i // blocks_per_seq, 0, i % blocks_per_seq, 0))
    else:
        shape = jax.ShapeDtypeStruct((n, d), dtype)
        spec = pl.BlockSpec((bm, d), lambda i: (i, 0))
    return shape, spec


def _proj_kernel(*refs, bm, bn, rope, perm_out):
    refs = list(refs)
    lhs_ref, w_ref = refs[:2]
    del refs[:2]
    if rope:
        cos_ref, sin_ref = refs[:2]
        del refs[:2]
    if perm_out:
        p_ref = refs.pop(0)
    o_ref = refs.pop(0)
    wbf_ref = refs.pop()

    @pl.when(pl.program_id(1) == 0)
    def _():
        wbf_ref[...] = w_ref[...].astype(BF16)

    acc = jnp.dot(lhs_ref[...], wbf_ref[...], preferred_element_type=F32)
    if rope:
        cos = cos_ref[...]
        sin = sin_ref[...]
        for hh in range(bn // HEAD_DIM):
            t = acc[:, hh * HEAD_DIM:(hh + 1) * HEAD_DIM]
            rot = pltpu.roll(t, HEAD_DIM // 2, 1)
            o_ref[:, hh * HEAD_DIM:(hh + 1) * HEAD_DIM] = (t * cos + rot * sin).astype(o_ref.dtype)
    else:
        o_ref[...] = acc.astype(o_ref.dtype)
    if perm_out:
        _store_res(refs[0], p_ref, o_ref[...])


def _proj_call(lhs, w, *, seq, col_off, n_cols, name, layer=0, tables=None, perm_mat=None,
               bm=1024, bn=1024):
    n, k = lhs.shape
    blocks_per_seq = seq // bm
    col_blk0 = col_off // bn
    rope, perm_out = tables is not None, perm_mat is not None
    args = [lhs, w]
    in_specs = [pl.BlockSpec((bm, k), lambda j, i: (i, 0)),
                pl.BlockSpec((None, k, bn), lambda j, i: (layer, 0, j + col_blk0))]
    if rope:
        args += list(tables)
        in_specs += [pl.BlockSpec((bm, HEAD_DIM), lambda j, i: (i % blocks_per_seq, 0))] * 2
    if perm_out:
        args.append(perm_mat)
        in_specs.append(_const_spec(perm_mat.shape, 2))
    out_shape = [jax.ShapeDtypeStruct((n, n_cols), BF16)]
    out_specs = [pl.BlockSpec((bm, bn), lambda j, i: (i, j))]
    if perm_out:
        out_shape.append(jax.ShapeDtypeStruct((n // seq, N_RES, seq // N_RES, n_cols), BF16))
        out_specs.append(pl.BlockSpec(
            (None, N_RES, bm // N_RES, bn),
            lambda j, i: (i // blocks_per_seq, 0, i % blocks_per_seq, j)))
    res = pl.pallas_call(
        functools.partial(_proj_kernel, bm=bm, bn=bn, rope=rope, perm_out=perm_out),
        grid=(n_cols // bn, n // bm),
        in_specs=in_specs,
        out_specs=out_specs,
        out_shape=out_shape,
        scratch_shapes=[pltpu.VMEM((k, bn), BF16)],
        compiler_params=_cparams(2),
        name=name,
    )(*args)
    return res if perm_out else res[0]


def _wcomb_kernel(wu_ref, grp_ref, scale_ref, o_ref):
    grp = (grp_ref[...] * scale_ref[...]).astype(BF16)
    o_ref[...] = jnp.dot(wu_ref[...].astype(BF16), grp, preferred_element_type=F32).astype(o_ref.dtype)


def _wcomb_call(w_in, w_grp, scale, *, layer, name, bk=1024):
    n_layers, k, e2 = w_in.shape
    n_grp, gc = w_grp.shape[1], w_grp.shape[2]
    return pl.pallas_call(
        _wcomb_kernel,
        grid=(n_grp, k // bk),
        in_specs=[pl.BlockSpec((None, bk, gc), lambda g, i: (layer, i, g)),
                  pl.BlockSpec((None, None, gc, gc), lambda g, i: (layer, g, 0, 0)),
                  pl.BlockSpec((None, 1, gc), lambda g, i: (layer * n_grp + g, 0, 0))],
        out_specs=pl.BlockSpec((bk, gc), lambda g, i: (i, g)),
        out_shape=jax.ShapeDtypeStruct((k, e2 // 2), BF16),
        compiler_params=_cparams(2),
        name=name,
    )(w_in, w_grp, scale.reshape(n_layers * n_grp, 1, gc))


def _ah_kernel(hdn_ref, wc_ref, wg_ref, *rest, bm, blocks_per_seq, normed):
    gain_ref = rest[0] if normed else None
    h_ref, wgbf_ref, carry_ref = rest[-3:]
    i = pl.program_id(0)
    blk = i % blocks_per_seq
    gc = h_ref.shape[1] // len(POOL_WINDOWS)

    @pl.when(i == 0)
    def _():
        wgbf_ref[...] = wg_ref[...].astype(BF16)

    @pl.when(blk == 0)
    def _():
        carry_ref[...] = jnp.zeros_like(carry_ref)

    if gain_ref is None:
        hdn = hdn_ref[...]
    else:
        xf = hdn_ref[...]
        inv = lax.rsqrt(jnp.mean(xf * xf, axis=-1, keepdims=True) + RMS_EPS)
        hdn = ((xf * inv) * gain_ref[...]).astype(BF16)
    pos = blk * bm + lax.broadcasted_iota(jnp.int32, (bm, 1), 0)

    def finish(g, acc_y, acc_g):
        w = POOL_WINDOWS[g]
        cols = slice(g * gc, (g + 1) * gc)
        ext = jnp.concatenate([carry_ref[:, cols], acc_y], axis=0)
        carry_ref[:, cols] = acc_y[bm - POOL_HALO:, :]
        wsum, have = ext, 1
        while have < w:
            wsum = wsum + pltpu.roll(wsum, have, 0)
            have *= 2
        inv_cnt = 1.0 / jnp.minimum(pos + 1, w).astype(F32)
        pooled = wsum[POOL_HALO:, :] * inv_cnt - acc_y
        h_ref[:, cols] = (pooled * (acc_g / (1.0 + jnp.exp(-acc_g)))).astype(h_ref.dtype)

    for g in reversed(range(len(POOL_WINDOWS))):
        cols = slice(g * gc, (g + 1) * gc)
        finish(g, jnp.dot(hdn, wc_ref[:, cols], preferred_element_type=F32),
               jnp.dot(hdn, wgbf_ref[:, cols], preferred_element_type=F32))


def _ah_call(hdn, w_comb, w_in, *, layer, seq, name, gain=None, bm=512):
    n, k = hdn.shape
    e = w_comb.shape[1]
    args = [hdn, w_comb, w_in]
    in_specs = [pl.BlockSpec((bm, k), lambda i: (i, 0)),
                _const_spec(w_comb.shape, 1),
                pl.BlockSpec((None, k, e), lambda i: (layer, 0, 1), pipeline_mode=pl.Buffered(1))]
    if gain is not None:
        args.append(gain.reshape(1, k))
        in_specs.append(_const_spec((1, k), 1))
    kern = functools.partial(_ah_kernel, bm=bm, blocks_per_seq=seq // bm, normed=gain is not None)
    return pl.pallas_call(
        kern,
        grid=(n // bm,),
        in_specs=in_specs,
        out_specs=pl.BlockSpec((bm, e), lambda i: (i, 0)),
        out_shape=jax.ShapeDtypeStruct((n, e), BF16),
        scratch_shapes=[pltpu.VMEM((k, e), BF16), pltpu.VMEM((POOL_HALO, e), F32)],
        compiler_params=_cparams(1),
        name=name,
    )(*args)


def _bout_kernel(hg_ref, x_ref, wout_ref, gains_ref, pres_ref, pblk_ref, *rest,
                 bm, res_in, emit_x, norm_specs):
    outs = rest
    steps = RES_ROWS // N_RES
    n_sub = bm // RES_ROWS
    hs = []
    for sub in range(n_sub):
        if not res_in:
            hs.append(hg_ref[sub * RES_ROWS:(sub + 1) * RES_ROWS, :])
            continue
        blk = jnp.concatenate(
            [hg_ref[r, sub * steps:(sub + 1) * steps, :] for r in range(N_RES)], axis=0)
        hs.append(jnp.dot(pres_ref[...], blk, preferred_element_type=F32).astype(BF16))
    xns = []
    for sub in range(n_sub):
        rows = slice(sub * RES_ROWS, (sub + 1) * RES_ROWS)
        xns.append(x_ref[rows, :] + jnp.dot(hs[sub], wout_ref[...], preferred_element_type=F32))
    for sub in range(n_sub):
        rows = slice(sub * RES_ROWS, (sub + 1) * RES_ROWS)
        if emit_x:
            outs[0][rows, :] = xns[sub]
        _emit_norms(xns[sub], gains_ref, pres_ref, pblk_ref, outs[1:] if emit_x else outs,
                    norm_specs, sub * RES_ROWS)


def _bout_call(hg, x, w_out, gains, pres, pblk, *, layer, seq, emit_x, norm_specs, name, bm=512):
    n, d = x.shape
    e = hg.shape[-1]
    blocks_per_seq = seq // bm
    res_in = hg.ndim == 4
    norm_shapes, norm_out_specs = zip(*[_norm_out(n, d, seq, bm, o, dt) for o, dt in norm_specs])
    out_specs, out_shape = list(norm_out_specs), list(norm_shapes)
    if emit_x:
        out_specs.insert(0, pl.BlockSpec((bm, d), lambda i: (i, 0)))
        out_shape.insert(0, jax.ShapeDtypeStruct((n, d), F32))
    kern = functools.partial(_bout_kernel, bm=bm, res_in=res_in, emit_x=emit_x,
                             norm_specs=tuple(norm_specs))
    if res_in:
        hg_spec = pl.BlockSpec((None, N_RES, bm // N_RES, e),
                               lambda i: (i // blocks_per_seq, 0, i % blocks_per_seq, 0))
    else:
        hg_spec = pl.BlockSpec((bm, e), lambda i: (i, 0))
    return pl.pallas_call(
        kern,
        grid=(n // bm,),
        in_specs=[hg_spec,
                  pl.BlockSpec((bm, d), lambda i: (i, 0)),
                  _const_spec(w_out.shape, 1, layer),
                  _const_spec(gains.shape, 1),
                  _const_spec(pres.shape, 1),
                  _const_spec(pblk.shape, 1)],
        out_specs=out_specs,
        out_shape=out_shape,
        compiler_params=_cparams(1),
        name=name,
    )(hg, x, w_out, gains, pres, pblk)


def _band_blocks(blocks):
    nt = (((1,), (1,)), ((), ()))
    ss = [lax.dot_general(q, k, nt, preferred_element_type=F32) for q, k, _, _ in blocks]
    ss = [jnp.where(blk[3], s, NEG_INF) for blk, s in zip(blocks, ss)]
    ms = [jnp.max(s, axis=-1, keepdims=True) for s in ss]
    ps = [jnp.exp2(s - m).astype(BF16) for s, m in zip(ss, ms)]
    pvs = [jnp.dot(p, jnp.concatenate([blk[2], jnp.ones_like(blk[2])], axis=1),
                   preferred_element_type=F32) for blk, p in zip(blocks, ps)]
    return [(pv[:, :HEAD_DIM], m, pv[:, HEAD_DIM:]) for pv, m in zip(pvs, ms)]


def _band_masks(row_idx, col_idx, row_idx1, col_idx1):
    band = (col_idx >= row_idx) & (col_idx <= row_idx + BAND)
    return band, col_idx1 <= row_idx1


def _attn_kernel(q0_ref, kn_ref, vn_ref, q1_ref, q2_ref, gate_ref, kp_ref, vp_ref, out_ref,
                 a0_s, m0_s, l0_s):
    seq = q0_ref.shape[0]
    steps = seq // N_RES
    row = lax.broadcasted_iota(jnp.int32, (BAND, 2 * BAND), 0)
    col = lax.broadcasted_iota(jnp.int32, (BAND, 2 * BAND), 1)
    row1 = lax.broadcasted_iota(jnp.int32, (BAND, BAND), 0)
    col1 = lax.broadcasted_iota(jnp.int32, (BAND, BAND), 1)
    mask_band, mask_first = _band_masks(row, col, row1, col1)
    blk_pos = lambda p: N_RES * (p % SUBLANES) + p // SUBLANES
    mask0_band, mask0_first = _band_masks(blk_pos(row), col, blk_pos(row1), col1)
    d4 = DILATIONS[1]
    chunk = BAND // d4
    n_sub = N_RES // d4
    d4_pos = lambda p: d4 * (p % chunk) + p // chunk
    mask4_band, mask4_first = _band_masks(
        d4_pos(row), d4_pos(col % BAND) + BAND * (col // BAND), d4_pos(row1), d4_pos(col1))

    def bcast(m):
        return jnp.broadcast_to(m, (m.shape[0], LANES))

    unroll0 = 16
    span = unroll0 * BAND
    tile = (N_RES, SUBLANES, LANES)

    def g0_group(it, first):
        base = 0 if first else pl.multiple_of(it * span, span)
        q_all = q0_ref[pl.ds(base, span), :]
        if first:
            k_all, v_all = kn_ref[0:span, :], vn_ref[0:span, :]
        else:
            lo = pl.multiple_of(base - BAND, BAND)
            k_all, v_all = kn_ref[pl.ds(lo, span + BAND), :], vn_ref[pl.ds(lo, span + BAND), :]
        blocks = []
        for u in range(unroll0):
            q = q_all[u * BAND:(u + 1) * BAND, :]
            if first and u == 0:
                blocks.append((q, k_all[0:BAND, :], v_all[0:BAND, :], mask0_first))
            else:
                k0 = (u - 1) * BAND if first else u * BAND
                blocks.append((q, k_all[k0:k0 + 2 * BAND, :], v_all[k0:k0 + 2 * BAND, :], mask0_band))
        for u, (acc, m, l) in enumerate(_band_blocks(blocks)):
            a0_s[it * unroll0 + u] = acc.reshape(tile)
            m0_s[it * unroll0 + u] = bcast(m).reshape(tile)
            l0_s[it * unroll0 + u] = l.reshape(tile)

    g0_group(0, True)

    def g0_body(it, carry):
        g0_group(it, False)
        return carry

    lax.fori_loop(1, seq // span, g0_body, 0)

    n_blk2 = steps // BAND
    n_blk1 = steps // chunk

    def class_blocks(r4):
        def rows_of(ref, a):
            return [ref[r4 + d4 * k, a * chunk:(a + 1) * chunk, :] for k in range(n_sub)]

        blocks = []
        for k in range(n_sub):
            r = r4 + d4 * k
            q_all, k_all, v_all = q2_ref[r], kp_ref[r], vp_ref[r]
            for nb in range(n_blk2):
                q = q_all[nb * BAND:(nb + 1) * BAND, :]
                if nb == 0:
                    blocks.append((q, k_all[0:BAND, :], v_all[0:BAND, :], mask_first))
                else:
                    k0 = (nb - 1) * BAND
                    blocks.append((q, k_all[k0:k0 + 2 * BAND, :], v_all[k0:k0 + 2 * BAND, :], mask_band))
        for a in range(n_blk1):
            q = jnp.concatenate(rows_of(q1_ref, a), axis=0)
            if a == 0:
                blocks.append((q, jnp.concatenate(rows_of(kp_ref, 0), axis=0),
                               jnp.concatenate(rows_of(vp_ref, 0), axis=0), mask4_first))
            else:
                blocks.append((q,
                               jnp.concatenate(rows_of(kp_ref, a - 1) + rows_of(kp_ref, a), axis=0),
                               jnp.concatenate(rows_of(vp_ref, a - 1) + rows_of(vp_ref, a), axis=0),
                               mask4_band))
        return _band_blocks(blocks)

    def class_merge(r4, res):
        res2, res1 = res[:n_sub * n_blk2], res[n_sub * n_blk2:]
        for k in range(n_sub):
            r = r4 + d4 * k
            rows1 = slice(k * chunk, (k + 1) * chunk)
            for nb in range(n_blk2):
                rows = slice(nb * BAND, (nb + 1) * BAND)
                blk0 = slice(nb * BAND // SUBLANES, (nb + 1) * BAND // SUBLANES)
                a0 = a0_s[blk0, r].reshape(BAND, LANES)
                m0 = m0_s[blk0, r].reshape(BAND, LANES)
                l0 = l0_s[blk0, r].reshape(BAND, LANES)
                part1 = res1[nb * BAND // chunk:(nb + 1) * BAND // chunk]
                a1 = jnp.concatenate([acc[rows1, :] for acc, _, _ in part1], axis=0)
                m1 = jnp.concatenate([bcast(m[rows1, :]) for _, m, _ in part1], axis=0)
                l1 = jnp.concatenate([l[rows1, :] for _, _, l in part1], axis=0)
                a2, m2, l2 = res2[k * n_blk2 + nb]
                m2 = bcast(m2)
                mx = jnp.maximum(jnp.maximum(m0, m1), m2)
                w0, w1, w2 = jnp.exp2(m0 - mx), jnp.exp2(m1 - mx), jnp.exp2(m2 - mx)
                num = w0 * a0 + w1 * a1 + w2 * a2
                den = w0 * l0 + w1 * l1 + w2 * l2
                gt = gate_ref[r, rows, :].astype(F32)
                out_ref[r, rows, :] = ((num * gt) / (den * (1.0 + jnp.exp(-gt)))).astype(out_ref.dtype)

    pending = class_blocks(0)
    for r4 in range(d4):
        upcoming = class_blocks(r4 + 1) if r4 + 1 < d4 else None
        class_merge(r4, pending)
        pending = upcoming


def _attn_call(q0, k_nat, v_nat, q12, gate, k_res, v_res, *, n_heads, name):
    b, seq, e = q0.shape
    steps = seq // N_RES
    nat = lambda off: pl.BlockSpec((None, seq, HEAD_DIM), lambda bi, h: (bi, 0, h + off))
    res = lambda off: pl.BlockSpec((None, N_RES, steps, HEAD_DIM), lambda bi, h: (bi, 0, 0, h + off))
    blk_scratch = pltpu.VMEM((seq // BLK_ROWS, N_RES, SUBLANES, LANES), F32)
    return pl.pallas_call(
        _attn_kernel,
        grid=(b, n_heads),
        in_specs=[nat(0), nat(0), nat(0), res(0), res(n_heads), res(0), res(0), res(0)],
        out_specs=res(0),
        out_shape=jax.ShapeDtypeStruct((b, N_RES, steps, e), BF16),
        scratch_shapes=[blk_scratch] * 3,
        compiler_params=_cparams(2),
        name=name,
    )(q0, k_nat, v_nat, q12, q12, gate, k_res, v_res)


def _rope_tables(seq):
    inv_freq = 1.0 / (ROPE_THETA ** (jnp.arange(0, HEAD_DIM, 2, dtype=F32) / HEAD_DIM))
    ang = jnp.arange(seq, dtype=F32)[:, None] * inv_freq[None, :]
    cos, sin = jnp.cos(ang), jnp.sin(ang)
    return jnp.concatenate([cos, cos], axis=-1), jnp.concatenate([-sin, sin], axis=-1)


def _to_res_order(table, seq):
    return table.reshape(seq // N_RES, N_RES, -1).transpose(1, 0, 2).reshape(seq, -1)


def _to_blk_order(table, seq):
    steps = BLK_ROWS // N_RES
    return table.reshape(seq // BLK_ROWS, steps, N_RES, -1).transpose(0, 2, 1, 3).reshape(seq, -1)


def _perm_matrix(n_rows):
    steps = n_rows // N_RES
    idx = jnp.arange(n_rows)
    src = (idx % steps) * N_RES + idx // steps
    return (src[:, None] == idx[None, :]).astype(BF16)


def kernel(x, norm_a, w_in_a, w_grp_a, scale_a, w_out_a, norm_kv, w_k, w_v, norm_b, w_in_b,
           w_out_b, norm_f):
    b, seq, d = x.shape
    n = b * seq
    n_a = w_in_a.shape[0]
    n_b = w_in_b.shape[0]
    e_b = w_k.shape[1]
    n_heads = e_b // HEAD_DIM
    assert seq % (N_RES * BAND) == 0 and d % HEAD_DIM == 0

    rope_nat = _rope_tables(seq)
    rope_res = tuple(_to_res_order(t, seq) * QK_SCALE for t in rope_nat)
    rope_blk = tuple(_to_blk_order(t, seq) * QK_SCALE for t in rope_nat)
    pres, pblk = _perm_matrix(RES_ROWS), _perm_matrix(BLK_ROWS)

    xf = x.reshape(n, d)
    hdn = None

    w_out_a_bf, w_out_b_bf = w_out_a.astype(BF16), w_out_b.astype(BF16)
    for i in range(n_a):
        w_comb = _wcomb_call(w_in_a, w_grp_a, scale_a, layer=i, name=f"a{i}_wcomb")
        if hdn is None:
            h = _ah_call(xf, w_comb, w_in_a, layer=i, seq=seq, gain=norm_a[i], name=f"a{i}_h")
        else:
            h = _ah_call(hdn, w_comb, w_in_a, layer=i, seq=seq, name=f"a{i}_h")
        if i < n_a - 1:
            gains = norm_a[i + 1][None]
            specs = [("nat", BF16)]
        else:
            gains = jnp.stack([norm_kv, norm_b[0], norm_b[0]])
            specs = [("nat", BF16), ("blk", BF16), ("res", BF16)]
        outs = _bout_call(h, xf, w_out_a_bf, gains, pres, pblk, layer=i, seq=seq, emit_x=True,
                          norm_specs=specs, name=f"a{i}_out")
        xf = outs[0]
        if i < n_a - 1:
            hdn = outs[1]
        else:
            hdn_kv, hdn_blk, hdn_res = outs[1], outs[2], outs[3]

    k_nat, k_res = _proj_call(hdn_kv, w_k[None], seq=seq, col_off=0, n_cols=e_b,
                              tables=rope_nat, perm_mat=pres, name="k")
    v_nat, v_res = _proj_call(hdn_kv, w_v[None], seq=seq, col_off=0, n_cols=e_b,
                              perm_mat=pres, name="v")
    k_nat, v_nat = k_nat.reshape(b, seq, e_b), v_nat.reshape(b, seq, e_b)

    out = None
    res_shape = lambda c: (b, N_RES, seq // N_RES, c)
    for i in range(n_b):
        hdn_res2 = hdn_res.reshape(n, d)
        q0 = _proj_call(hdn_blk, w_in_b, layer=i, seq=seq, col_off=0, n_cols=e_b, tables=rope_blk,
                        name=f"b{i}_q0")
        q12 = _proj_call(hdn_res2, w_in_b, layer=i, seq=seq, col_off=e_b, n_cols=2 * e_b,
                         tables=rope_res, name=f"b{i}_q12")
        gate = _proj_call(hdn_res2, w_in_b, layer=i, seq=seq, col_off=3 * e_b, n_cols=e_b,
                          name=f"b{i}_gate")
        hg = _attn_call(q0.reshape(b, seq, e_b), k_nat, v_nat, q12.reshape(res_shape(2 * e_b)),
                        gate.reshape(res_shape(e_b)), k_res, v_res, n_heads=n_heads,
                        name=f"b{i}_attn")
        if i < n_b - 1:
            gains = jnp.stack([norm_b[i + 1], norm_b[i + 1]])
            xf, hdn_blk, hdn_res = _bout_call(
                hg, xf, w_out_b_bf, gains, pres, pblk, layer=i, seq=seq, emit_x=True,
                norm_specs=[("blk", BF16), ("res", BF16)], name=f"b{i}_out")
        else:
            (out,) = _bout_call(hg, xf, w_out_b_bf, norm_f[None], pres, pblk, layer=i, seq=seq,
                                emit_x=False, norm_specs=[("nat", F32)], name=f"b{i}_out")
    return out.reshape(b, seq, d)
```

```python
import functools
import math

import jax
import jax.numpy as jnp
import numpy as np
from jax import lax
from jax.experimental import pallas as pl
from jax.experimental.pallas import tpu as pltpu

F32 = jnp.float32
BF16 = jnp.bfloat16

RMS_EPS = 1e-6
POOL_WINDOWS = (2, 4, 8, 16)
POOL_HALO = 16
HEAD_DIM = 128
ROPE_THETA = 10000.0
NEG_INF = -1e30
N_RES = 16
DILATIONS = (1, 4, 16)
BAND = 128
SUBLANES = 8
LANES = 128
RES_ROWS = N_RES * N_RES
SUB_ROWS = RES_ROWS
BLK_ROWS = BAND
VMEM_LIMIT = 56 * 1024 * 1024
QK_SCALE = math.log2(math.e) / math.sqrt(HEAD_DIM)


def _cparams(n_axes):
    return pltpu.CompilerParams(
        dimension_semantics=("arbitrary",) * n_axes, vmem_limit_bytes=VMEM_LIMIT)


def _const_spec(shape, n_grid, layer=None):
    if layer is None:
        block, index = shape, (0,) * len(shape)
    else:
        block, index = (None,) + tuple(shape[1:]), (layer,) + (0,) * (len(shape) - 1)
    if n_grid == 1:
        return pl.BlockSpec(block, lambda i: index, pipeline_mode=pl.Buffered(1))
    return pl.BlockSpec(block, lambda j, i: index, pipeline_mode=pl.Buffered(1))


def _store_res(dst_ref, p_ref, rows, row0=0):
    steps = RES_ROWS // N_RES
    for sub in range(rows.shape[0] // RES_ROWS):
        pb = jnp.dot(p_ref[...], rows[sub * RES_ROWS:(sub + 1) * RES_ROWS, :],
                     preferred_element_type=F32).astype(dst_ref.dtype)
        s0 = row0 // N_RES + sub * steps
        for r in range(N_RES):
            dst_ref[r, s0:s0 + steps, :] = pb[r * steps:(r + 1) * steps, :]


def _emit_norms(xn, gains_ref, pres_ref, pblk_ref, out_refs, norm_specs, row0):
    inv = lax.rsqrt(jnp.mean(xn * xn, axis=-1, keepdims=True) + RMS_EPS)
    xh = xn * inv
    for k, (order, dtype) in enumerate(norm_specs):
        hd = (xh * gains_ref[k:k + 1, :]).astype(dtype)
        if order == "nat":
            out_refs[k][row0:row0 + xn.shape[0], :] = hd
        elif order == "res":
            _store_res(out_refs[k], pres_ref, hd, row0)
        else:
            for sub in range(xn.shape[0] // BLK_ROWS):
                src = slice(sub * BLK_ROWS, (sub + 1) * BLK_ROWS)
                dst = slice(row0 + sub * BLK_ROWS, row0 + (sub + 1) * BLK_ROWS)
                out_refs[k][dst, :] = jnp.dot(pblk_ref[...], hd[src, :],
                                              preferred_element_type=F32).astype(dtype)


def _norm_out(n, d, seq, bm, order, dtype):
    blocks_per_seq = seq // bm
    if order == "res":
        shape = jax.ShapeDtypeStruct((n // seq, N_RES, seq // N_RES, d), dtype)
        spec = pl.BlockSpec((None, N_RES, bm // N_RES, d),
                            lambda i: (i // blocks_per_seq, 0, i % blocks_per_seq, 0))
    else:
        shape = jax.ShapeDtypeStruct((n, d), dtype)
        spec = pl.BlockSpec((bm, d), lambda i: (i, 0))
    return shape, spec


def _proj_kernel(*refs, bm, bn, rope, perm_out):
    refs = list(refs)
    lhs_ref, w_ref = refs[:2]
    del refs[:2]
    if rope:
        cos_ref, sin_ref = refs[:2]
        del refs[:2]
    if perm_out:
        p_ref = refs.pop(0)
    o_ref = refs.pop(0)
    wbf_ref = refs.pop()

    @pl.when(pl.program_id(1) == 0)
    def _():
        wbf_ref[...] = w_ref[...].astype(BF16)

    acc = jnp.dot(lhs_ref[...], wbf_ref[...], preferred_element_type=F32)
    if rope:
        cos = cos_ref[...]
        sin = sin_ref[...]
        for hh in range(bn // HEAD_DIM):
            t = acc[:, hh * HEAD_DIM:(hh + 1) * HEAD_DIM]
            rot = pltpu.roll(t, HEAD_DIM // 2, 1)
            o_ref[:, hh * HEAD_DIM:(hh + 1) * HEAD_DIM] = (t * cos + rot * sin).astype(o_ref.dtype)
    else:
        o_ref[...] = acc.astype(o_ref.dtype)
    if perm_out:
        _store_res(refs[0], p_ref, o_ref[...])


def _proj_call(lhs, w, *, seq, col_off, n_cols, name, layer=0, tables=None, perm_mat=None,
               bm=1024, bn=1024):
    n, k = lhs.shape
    blocks_per_seq = seq // bm
    col_blk0 = col_off // bn
    rope, perm_out = tables is not None, perm_mat is not None
    args = [lhs, w]
    in_specs = [pl.BlockSpec((bm, k), lambda j, i: (i, 0)),
                pl.BlockSpec((None, k, bn), lambda j, i: (layer, 0, j + col_blk0))]
    if rope:
        args += list(tables)
        in_specs += [pl.BlockSpec((bm, HEAD_DIM), lambda j, i: (i % blocks_per_seq, 0))] * 2
    if perm_out:
        args.append(perm_mat)
        in_specs.append(_const_spec(perm_mat.shape, 2))
    out_shape = [jax.ShapeDtypeStruct((n, n_cols), BF16)]
    out_specs = [pl.BlockSpec((bm, bn), lambda j, i: (i, j))]
    if perm_out:
        out_shape.append(jax.ShapeDtypeStruct((n // seq, N_RES, seq // N_RES, n_cols), BF16))
        out_specs.append(pl.BlockSpec(
            (None, N_RES, bm // N_RES, bn),
            lambda j, i: (i // blocks_per_seq, 0, i % blocks_per_seq, j)))
    res = pl.pallas_call(
        functools.partial(_proj_kernel, bm=bm, bn=bn, rope=rope, perm_out=perm_out),
        grid=(n_cols // bn, n // bm),
        in_specs=in_specs,
        out_specs=out_specs,
        out_shape=out_shape,
        scratch_shapes=[pltpu.VMEM((k, bn), BF16)],
        compiler_params=_cparams(2),
        name=name,
    )(*args)
    return res if perm_out else res[0]


def _wcomb_kernel(wu_ref, wg_ref, grp_ref, scale_ref, wc_ref, wgbf_ref):
    grp = (grp_ref[...] * scale_ref[...]).astype(BF16)
    wc_ref[...] = jnp.dot(wu_ref[...].astype(BF16), grp, preferred_element_type=F32).astype(BF16)
    wgbf_ref[...] = wg_ref[...].astype(BF16)


def _wcomb_call(w_in, w_grp, scale, *, layer, name, bk=1024):
    n_layers, k, e2 = w_in.shape
    n_grp, gc = w_grp.shape[1], w_grp.shape[2]
    e = e2 // 2
    w_shape = jax.ShapeDtypeStruct((k, e), BF16)
    return pl.pallas_call(
        _wcomb_kernel,
        grid=(n_grp, k // bk),
        in_specs=[pl.BlockSpec((None, bk, gc), lambda g, i: (layer, i, g)),
                  pl.BlockSpec((None, bk, gc), lambda g, i: (layer, i, n_grp + g)),
                  pl.BlockSpec((None, None, gc, gc), lambda g, i: (layer, g, 0, 0)),
                  pl.BlockSpec((None, 1, gc), lambda g, i: (layer * n_grp + g, 0, 0))],
        out_specs=[pl.BlockSpec((bk, gc), lambda g, i: (i, g))] * 2,
        out_shape=[w_shape, w_shape],
        compiler_params=_cparams(2),
        name=name,
    )(w_in, w_in, w_grp, scale.reshape(n_layers * n_grp, 1, gc))


def _ah_kernel(hdn_ref, wc_ref, wg_ref, *rest, bm, blocks_per_seq, normed):
    gain_ref = rest[0] if normed else None
    h_ref, carry_ref = rest[-2:]
    blk = pl.program_id(0) % blocks_per_seq
    gc = h_ref.shape[1] // len(POOL_WINDOWS)

    @pl.when(blk == 0)
    def _():
        carry_ref[...] = jnp.zeros_like(carry_ref)

    if gain_ref is None:
        hdn = hdn_ref[...]
    else:
        xf = hdn_ref[...]
        inv = lax.rsqrt(jnp.mean(xf * xf, axis=-1, keepdims=True) + RMS_EPS)
        hdn = ((xf * inv) * gain_ref[...]).astype(BF16)
    pos = blk * bm + lax.broadcasted_iota(jnp.int32, (bm, 1), 0)

    def finish(g, acc_y, acc_g):
        w = POOL_WINDOWS[g]
        cols = slice(g * gc, (g + 1) * gc)
        ext = jnp.concatenate([carry_ref[:, cols], acc_y], axis=0)
        carry_ref[:, cols] = acc_y[bm - POOL_HALO:, :]
        wsum, have = ext, 1
        while have < w:
            wsum = wsum + pltpu.roll(wsum, have, 0)
            have *= 2
        inv_cnt = 1.0 / jnp.minimum(pos + 1, w).astype(F32)
        pooled = wsum[POOL_HALO:, :] * inv_cnt - acc_y
        h_ref[:, cols] = (pooled * (acc_g / (1.0 + jnp.exp(-acc_g)))).astype(h_ref.dtype)

    for g in reversed(range(len(POOL_WINDOWS))):
        cols = slice(g * gc, (g + 1) * gc)
        finish(g, jnp.dot(hdn, wc_ref[:, cols], preferred_element_type=F32),
               jnp.dot(hdn, wg_ref[:, cols], preferred_element_type=F32))


def _ah_call(hdn, w_comb, w_gate, *, seq, name, gain=None, bm=1024):
    n, k = hdn.shape
    e = w_comb.shape[1]
    args = [hdn, w_comb, w_gate]
    in_specs = [pl.BlockSpec((bm, k), lambda i: (i, 0)),
                _const_spec(w_comb.shape, 1),
                _const_spec(w_gate.shape, 1)]
    if gain is not None:
        args.append(gain.reshape(1, k))
        in_specs.append(_const_spec((1, k), 1))
    kern = functools.partial(_ah_kernel, bm=bm, blocks_per_seq=seq // bm, normed=gain is not None)
    return pl.pallas_call(
        kern,
        grid=(n // bm,),
        in_specs=in_specs,
        out_specs=pl.BlockSpec((bm, e), lambda i: (i, 0)),
        out_shape=jax.ShapeDtypeStruct((n, e), BF16),
        scratch_shapes=[pltpu.VMEM((POOL_HALO, e), F32)],
        compiler_params=_cparams(1),
        name=name,
    )(*args)


def _bout_kernel(hg_ref, x_ref, wout_ref, gains_ref, pres_ref, pblk_ref, *rest,
                 bm, res_in, emit_x, norm_specs):
    outs = rest
    steps = RES_ROWS // N_RES
    n_sub = bm // RES_ROWS
    hs = []
    for sub in range(n_sub):
        if not res_in:
            hs.append(hg_ref[sub * RES_ROWS:(sub + 1) * RES_ROWS, :])
            continue
        blk = jnp.concatenate(
            [hg_ref[r, sub * steps:(sub + 1) * steps, :] for r in range(N_RES)], axis=0)
        hs.append(jnp.dot(pres_ref[...], blk, preferred_element_type=F32).astype(BF16))
    xns = []
    for sub in range(n_sub):
        rows = slice(sub * RES_ROWS, (sub + 1) * RES_ROWS)
        xns.append(x_ref[rows, :] + jnp.dot(hs[sub], wout_ref[...], preferred_element_type=F32))
    for sub in range(n_sub):
        rows = slice(sub * RES_ROWS, (sub + 1) * RES_ROWS)
        if emit_x:
            outs[0][rows, :] = xns[sub]
        _emit_norms(xns[sub], gains_ref, pres_ref, pblk_ref, outs[1:] if emit_x else outs,
                    norm_specs, sub * RES_ROWS)


def _bout_call(hg, x, w_out, gains, pres, pblk, *, layer, seq, emit_x, norm_specs, name, bm=512):
    n, d = x.shape
    e = hg.shape[-1]
    blocks_per_seq = seq // bm
    res_in = hg.ndim == 4
    norm_shapes, norm_out_specs = zip(*[_norm_out(n, d, seq, bm, o, dt) for o, dt in norm_specs])
    out_specs, out_shape = list(norm_out_specs), list(norm_shapes)
    if emit_x:
        out_specs.insert(0, pl.BlockSpec((bm, d), lambda i: (i, 0)))
        out_shape.insert(0, jax.ShapeDtypeStruct((n, d), F32))
    kern = functools.partial(_bout_kernel, bm=bm, res_in=res_in, emit_x=emit_x,
                             norm_specs=tuple(norm_specs))
    if res_in:
        hg_spec = pl.BlockSpec((None, N_RES, bm // N_RES, e),
                               lambda i: (i // blocks_per_seq, 0, i % blocks_per_seq, 0))
    else:
        hg_spec = pl.BlockSpec((bm, e), lambda i: (i, 0))
    return pl.pallas_call(
        kern,
        grid=(n // bm,),
        in_specs=[hg_spec,
                  pl.BlockSpec((bm, d), lambda i: (i, 0)),
                  _const_spec(w_out.shape, 1, layer),
                  _const_spec(gains.shape, 1),
                  _const_spec(pres.shape, 1),
                  _const_spec(pblk.shape, 1)],
        out_specs=out_specs,
        out_shape=out_shape,
        compiler_params=_cparams(1),
        name=name,
    )(hg, x, w_out, gains, pres, pblk)


def _band_blocks(blocks):
    nt = (((1,), (1,)), ((), ()))
    ss = [lax.dot_general(q, k, nt, preferred_element_type=F32) for q, k, _, _ in blocks]
    ss = [jnp.where(blk[3], s, NEG_INF) for blk, s in zip(blocks, ss)]
    ms = [jnp.max(s, axis=-1, keepdims=True) for s in ss]
    ps = [jnp.exp2(s - m).astype(BF16) for s, m in zip(ss, ms)]
    pvs = [jnp.dot(p, jnp.concatenate([blk[2], jnp.ones_like(blk[2])], axis=1),
                   preferred_element_type=F32) for blk, p in zip(blocks, ps)]
    return [(pv[:, :HEAD_DIM], m, pv[:, HEAD_DIM:]) for pv, m in zip(pvs, ms)]


def _band_masks(row_idx, col_idx, row_idx1, col_idx1):
    band = (col_idx >= row_idx) & (col_idx <= row_idx + BAND)
    return band, col_idx1 <= row_idx1


def _attn_kernel(q0_ref, kn_ref, vn_ref, q1_ref, q2_ref, gate_ref, kp_ref, vp_ref, out_ref,
                 a0_s, m0_s, l0_s):
    seq = q0_ref.shape[0]
    steps = seq // N_RES
    row = lax.broadcasted_iota(jnp.int32, (BAND, 2 * BAND), 0)
    col = lax.broadcasted_iota(jnp.int32, (BAND, 2 * BAND), 1)
    row1 = lax.broadcasted_iota(jnp.int32, (BAND, BAND), 0)
    col1 = lax.broadcasted_iota(jnp.int32, (BAND, BAND), 1)
    mask_band, mask_first = _band_masks(row, col, row1, col1)
    blk_pos = lambda p: N_RES * (p % SUBLANES) + p // SUBLANES
    mask0_band, mask0_first = _band_masks(blk_pos(row), col, blk_pos(row1), col1)
    d4 = DILATIONS[1]
    chunk = BAND // d4
    n_sub = N_RES // d4
    d4_pos = lambda p: d4 * (p % chunk) + p // chunk
    mask4_band, mask4_first = _band_masks(
        d4_pos(row), d4_pos(col % BAND) + BAND * (col // BAND), d4_pos(row1), d4_pos(col1))

    def bcast(m):
        return jnp.broadcast_to(m, (m.shape[0], LANES))

    unroll0 = 16
    span = unroll0 * BAND
    tile = (N_RES, SUBLANES, LANES)

    def g0_group(it, first):
        base = 0 if first else pl.multiple_of(it * span, span)
        q_all = q0_ref[pl.ds(base, span), :]
        if first:
            k_all, v_all = kn_ref[0:span, :], vn_ref[0:span, :]
        else:
            lo = pl.multiple_of(base - BAND, BAND)
            k_all, v_all = kn_ref[pl.ds(lo, span + BAND), :], vn_ref[pl.ds(lo, span + BAND), :]
        blocks = []
        for u in range(unroll0):
            q = q_all[u * BAND:(u + 1) * BAND, :]
            if first and u == 0:
                blocks.append((q, k_all[0:BAND, :], v_all[0:BAND, :], mask0_first))
            else:
                k0 = (u - 1) * BAND if first else u * BAND
                blocks.append((q, k_all[k0:k0 + 2 * BAND, :], v_all[k0:k0 + 2 * BAND, :], mask0_band))
        for u, (acc, m, l) in enumerate(_band_blocks(blocks)):
            a0_s[it * unroll0 + u] = acc.reshape(tile)
            m0_s[it * unroll0 + u] = bcast(m).reshape(tile)
            l0_s[it * unroll0 + u] = l.reshape(tile)

    g0_group(0, True)

    def g0_body(it, carry):
        g0_group(it, False)
        return carry

    lax.fori_loop(1, seq // span, g0_body, 0)

    n_blk2 = steps // BAND
    n_blk1 = steps // chunk

    def class_blocks(r4):
        def rows_of(ref, a):
            return [ref[r4 + d4 * k, a * chunk:(a + 1) * chunk, :] for k in range(n_sub)]

        blocks = []
        for k in range(n_sub):
            r = r4 + d4 * k
            q_all, k_all, v_all = q2_ref[r], kp_ref[r], vp_ref[r]
            for nb in range(n_blk2):
                q = q_all[nb * BAND:(nb + 1) * BAND, :]
                if nb == 0:
                    blocks.append((q, k_all[0:BAND, :], v_all[0:BAND, :], mask_first))
                else:
                    k0 = (nb - 1) * BAND
                    blocks.append((q, k_all[k0:k0 + 2 * BAND, :], v_all[k0:k0 + 2 * BAND, :], mask_band))
        for a in range(n_blk1):
            q = jnp.concatenate(rows_of(q1_ref, a), axis=0)
            if a == 0:
                blocks.append((q, jnp.concatenate(rows_of(kp_ref, 0), axis=0),
                               jnp.concatenate(rows_of(vp_ref, 0), axis=0), mask4_first))
            else:
                blocks.append((q,
                               jnp.concatenate(rows_of(kp_ref, a - 1) + rows_of(kp_ref, a), axis=0),
                               jnp.concatenate(rows_of(vp_ref, a - 1) + rows_of(vp_ref, a), axis=0),
                               mask4_band))
        return _band_blocks(blocks)

    def class_merge(r4, res):
        res2, res1 = res[:n_sub * n_blk2], res[n_sub * n_blk2:]
        for k in range(n_sub):
            r = r4 + d4 * k
            rows1 = slice(k * chunk, (k + 1) * chunk)
            for nb in range(n_blk2):
                rows = slice(nb * BAND, (nb + 1) * BAND)
                blk0 = slice(nb * BAND // SUBLANES, (nb + 1) * BAND // SUBLANES)
                a0 = a0_s[blk0, r].reshape(BAND, LANES)
                m0 = m0_s[blk0, r].reshape(BAND, LANES)
                l0 = l0_s[blk0, r].reshape(BAND, LANES)
                part1 = res1[nb * BAND // chunk:(nb + 1) * BAND // chunk]
                a1 = jnp.concatenate([acc[rows1, :] for acc, _, _ in part1], axis=0)
                m1 = jnp.concatenate([bcast(m[rows1, :]) for _, m, _ in part1], axis=0)
                l1 = jnp.concatenate([l[rows1, :] for _, _, l in part1], axis=0)
                a2, m2, l2 = res2[k * n_blk2 + nb]
                m2 = bcast(m2)
                mx = jnp.maximum(jnp.maximum(m0, m1), m2)
                w0, w1, w2 = jnp.exp2(m0 - mx), jnp.exp2(m1 - mx), jnp.exp2(m2 - mx)
                num = w0 * a0 + w1 * a1 + w2 * a2
                den = w0 * l0 + w1 * l1 + w2 * l2
                gt = gate_ref[r, rows, :].astype(F32)
                out_ref[r, rows, :] = ((num * gt) / (den * (1.0 + jnp.exp(-gt)))).astype(out_ref.dtype)

    pending = class_blocks(0)
    for r4 in range(d4):
        upcoming = class_blocks(r4 + 1) if r4 + 1 < d4 else None
        class_merge(r4, pending)
        pending = upcoming


def _attn_call(q0, k_nat, v_nat, q12, gate, k_res, v_res, *, n_heads, name):
    b, seq, e = q0.shape
    steps = seq // N_RES
    nat = lambda off: pl.BlockSpec((None, seq, HEAD_DIM), lambda bi, h: (bi, 0, h + off))
    res = lambda off: pl.BlockSpec((None, N_RES, steps, HEAD_DIM), lambda bi, h: (bi, 0, 0, h + off))
    blk_scratch = pltpu.VMEM((seq // BLK_ROWS, N_RES, SUBLANES, LANES), F32)
    return pl.pallas_call(
        _attn_kernel,
        grid=(b, n_heads),
        in_specs=[nat(0), nat(0), nat(0), res(0), res(n_heads), res(0), res(0), res(0)],
        out_specs=res(0),
        out_shape=jax.ShapeDtypeStruct((b, N_RES, steps, e), BF16),
        scratch_shapes=[blk_scratch] * 3,
        compiler_params=_cparams(2),
        name=name,
    )(q0, k_nat, v_nat, q12, q12, gate, k_res, v_res)


def _rope_tables(seq):
    inv_freq = 1.0 / (ROPE_THETA ** (jnp.arange(0, HEAD_DIM, 2, dtype=F32) / HEAD_DIM))
    ang = jnp.arange(seq, dtype=F32)[:, None] * inv_freq[None, :]
    cos, sin = jnp.cos(ang), jnp.sin(ang)
    return jnp.concatenate([cos, cos], axis=-1), jnp.concatenate([-sin, sin], axis=-1)


def _to_res_order(table, seq):
    return table.reshape(seq // N_RES, N_RES, -1).transpose(1, 0, 2).reshape(seq, -1)


def _to_blk_order(table, seq):
    steps = BLK_ROWS // N_RES
    return table.reshape(seq // BLK_ROWS, steps, N_RES, -1).transpose(0, 2, 1, 3).reshape(seq, -1)


def _perm_matrix(n_rows):
    steps = n_rows // N_RES
    idx = np.arange(n_rows)
    src = (idx % steps) * N_RES + idx // steps
    return jnp.asarray(src[:, None] == idx[None, :], dtype=BF16)


def kernel(x, norm_a, w_in_a, w_grp_a, scale_a, w_out_a, norm_kv, w_k, w_v, norm_b, w_in_b,
           w_out_b, norm_f):
    b, seq, d = x.shape
    n = b * seq
    n_a = w_in_a.shape[0]
    n_b = w_in_b.shape[0]
    e_b = w_k.shape[1]
    n_heads = e_b // HEAD_DIM
    assert seq % (N_RES * BAND) == 0 and d % HEAD_DIM == 0

    rope_nat = _rope_tables(seq)
    rope_res = tuple(_to_res_order(t, seq) * QK_SCALE for t in rope_nat)
    rope_blk = tuple(_to_blk_order(t, seq) * QK_SCALE for t in rope_nat)
    pres, pblk = _perm_matrix(RES_ROWS), _perm_matrix(BLK_ROWS)

    xf = x.reshape(n, d)
    hdn = None

    w_out_a_bf, w_out_b_bf = w_out_a.astype(BF16), w_out_b.astype(BF16)
    for i in range(n_a):
        w_comb, w_gate = _wcomb_call(w_in_a, w_grp_a, scale_a, layer=i, name=f"a{i}_wcomb")
        if hdn is None:
            h = _ah_call(xf, w_comb, w_gate, seq=seq, gain=norm_a[i], name=f"a{i}_h")
        else:
            h = _ah_call(hdn, w_comb, w_gate, seq=seq, name=f"a{i}_h")
        if i < n_a - 1:
            gains = norm_a[i + 1][None]
            specs = [("nat", BF16)]
        else:
            gains = jnp.stack([norm_kv, norm_b[0], norm_b[0]])
            specs = [("nat", BF16), ("blk", BF16), ("res", BF16)]
        outs = _bout_call(h, xf, w_out_a_bf, gains, pres, pblk, layer=i, seq=seq, emit_x=True,
                          norm_specs=specs, name=f"a{i}_out")
        xf = outs[0]
        if i < n_a - 1:
            hdn = outs[1]
        else:
            hdn_kv, hdn_blk, hdn_res = outs[1], outs[2], outs[3]

    k_nat, k_res = _proj_call(hdn_kv, w_k[None], seq=seq, col_off=0, n_cols=e_b,
                              tables=rope_nat, perm_mat=pres, name="k")
    v_nat, v_res = _proj_call(hdn_kv, w_v[None], seq=seq, col_off=0, n_cols=e_b,
                              perm_mat=pres, name="v")
    k_nat, v_nat = k_nat.reshape(b, seq, e_b), v_nat.reshape(b, seq, e_b)

    out = None
    res_shape = lambda c: (b, N_RES, seq // N_RES, c)
    for i in range(n_b):
        hdn_res2 = hdn_res.reshape(n, d)
        q0 = _proj_call(hdn_blk, w_in_b, layer=i, seq=seq, col_off=0, n_cols=e_b, tables=rope_blk,
                        name=f"b{i}_q0")
        q12 = _proj_call(hdn_res2, w_in_b, layer=i, seq=seq, col_off=e_b, n_cols=2 * e_b,
                         tables=rope_res, name=f"b{i}_q12")
        gate = _proj_call(hdn_res2, w_in_b, layer=i, seq=seq, col_off=3 * e_b, n_cols=e_b,
                          name=f"b{i}_gate")
        hg = _attn_call(q0.reshape(b, seq, e_b), k_nat, v_nat, q12.reshape(res_shape(2 * e_b)),
                        gate.reshape(res_shape(e_b)), k_res, v_res, n_heads=n_heads,
                        name=f"b{i}_attn")
        if i < n_b - 1:
            gains = jnp.stack([norm_b[i + 1], norm_b[i + 1]])
            xf, hdn_blk, hdn_res = _bout_call(
                hg, xf, w_out_b_bf, gains, pres, pblk, layer=i, seq=seq, emit_x=True,
                norm_specs=[("blk", BF16), ("res", BF16)], name=f"b{i}_out")
        else:
            (out,) = _bout_call(hg, xf, w_out_b_bf, norm_f[None], pres, pblk, layer=i, seq=seq,
                                emit_x=False, norm_specs=[("nat", F32)], name=f"b{i}_out")
    return out.reshape(b, seq, d)
```

```python
import functools
import math

import jax
import jax.numpy as jnp
import numpy as np
from jax import lax
from jax.experimental import pallas as pl
from jax.experimental.pallas import tpu as pltpu

F32 = jnp.float32
BF16 = jnp.bfloat16

RMS_EPS = 1e-6
POOL_WINDOWS = (2, 4, 8, 16)
POOL_HALO = 16
HEAD_DIM = 128
ROPE_THETA = 10000.0
NEG_INF = -1e30
N_RES = 16
DILATIONS = (1, 4, 16)
BAND = 128
SUBLANES = 8
LANES = 128
RES_ROWS = N_RES * N_RES
SUB_ROWS = RES_ROWS
BLK_ROWS = BAND
VMEM_LIMIT = 56 * 1024 * 1024
QK_SCALE = math.log2(math.e) / math.sqrt(HEAD_DIM)


def _cparams(n_axes):
    return pltpu.CompilerParams(
        dimension_semantics=("arbitrary",) * n_axes, vmem_limit_bytes=VMEM_LIMIT)


def _const_spec(shape, n_grid, layer=None):
    if layer is None:
        block, index = shape, (0,) * len(shape)
    else:
        block, index = (None,) + tuple(shape[1:]), (layer,) + (0,) * (len(shape) - 1)
    if n_grid == 1:
        return pl.BlockSpec(block, lambda i: index, pipeline_mode=pl.Buffered(1))
    return pl.BlockSpec(block, lambda j, i: index, pipeline_mode=pl.Buffered(1))


def _store_res(dst_ref, p_ref, rows, row0=0):
    steps = RES_ROWS // N_RES
    for sub in range(rows.shape[0] // RES_ROWS):
        pb = jnp.dot(p_ref[...], rows[sub * RES_ROWS:(sub + 1) * RES_ROWS, :],
                     preferred_element_type=F32).astype(dst_ref.dtype)
        s0 = row0 // N_RES + sub * steps
        for r in range(N_RES):
            dst_ref[r, s0:s0 + steps, :] = pb[r * steps:(r + 1) * steps, :]


def _emit_norms(xn, gains_ref, pres_ref, pblk_ref, out_refs, norm_specs, row0):
    inv = lax.rsqrt(jnp.mean(xn * xn, axis=-1, keepdims=True) + RMS_EPS)
    xh = xn * inv
    for k, (order, dtype) in enumerate(norm_specs):
        hd = (xh * gains_ref[k:k + 1, :]).astype(dtype)
        if order == "nat":
            out_refs[k][row0:row0 + xn.shape[0], :] = hd
        elif order == "res":
            _store_res(out_refs[k], pres_ref, hd, row0)
        else:
            for sub in range(xn.shape[0] // BLK_ROWS):
                src = slice(sub * BLK_ROWS, (sub + 1) * BLK_ROWS)
                dst = slice(row0 + sub * BLK_ROWS, row0 + (sub + 1) * BLK_ROWS)
                out_refs[k][dst, :] = jnp.dot(pblk_ref[...], hd[src, :],
                                              preferred_element_type=F32).astype(dtype)


def _norm_out(n, d, seq, bm, order, dtype):
    blocks_per_seq = seq // bm
    if order == "res":
        shape = jax.ShapeDtypeStruct((n // seq, N_RES, seq // N_RES, d), dtype)
        spec = pl.BlockSpec((None, N_RES, bm // N_RES, d),
                            lambda i: (i // blocks_per_seq, 0, i % blocks_per_seq, 0))
    else:
        shape = jax.ShapeDtypeStruct((n, d), dtype)
        spec = pl.BlockSpec((bm, d), lambda i: (i, 0))
    return shape, spec


def _proj_kernel(*refs, bm, bn, rope, perm_out):
    refs = list(refs)
    lhs_ref, w_ref = refs[:2]
    del refs[:2]
    if rope:
        cos_ref, sin_ref = refs[:2]
        del refs[:2]
    if perm_out:
        p_ref = refs.pop(0)
    o_ref = refs.pop(0)
    wbf_ref = refs.pop()

    @pl.when(pl.program_id(1) == 0)
    def _():
        wbf_ref[...] = w_ref[...].astype(BF16)

    sub_rows = bm if perm_out else SUB_ROWS
    for r0 in range(0, bm, sub_rows):
        rows = slice(r0, r0 + sub_rows)
        acc = jnp.dot(lhs_ref[rows, :], wbf_ref[...], preferred_element_type=F32)
        if rope:
            cos = cos_ref[rows, :]
            sin = sin_ref[rows, :]
            heads = []
            for hh in range(bn // HEAD_DIM):
                t = acc[:, hh * HEAD_DIM:(hh + 1) * HEAD_DIM]
                rot = pltpu.roll(t, HEAD_DIM // 2, 1)
                heads.append((t * cos + rot * sin).astype(o_ref.dtype))
            res = jnp.concatenate(heads, axis=1)
        else:
            res = acc.astype(o_ref.dtype)
        o_ref[rows, :] = res
        if perm_out:
            _store_res(refs[0], p_ref, res, r0)


def _proj_call(lhs, w, *, seq, col_off, n_cols, name, layer=0, tables=None, perm_mat=None,
               bm=1024, bn=1024):
    n, k = lhs.shape
    blocks_per_seq = seq // bm
    col_blk0 = col_off // bn
    rope, perm_out = tables is not None, perm_mat is not None
    args = [lhs, w]
    in_specs = [pl.BlockSpec((bm, k), lambda j, i: (i, 0)),
                pl.BlockSpec((None, k, bn), lambda j, i: (layer, 0, j + col_blk0))]
    if rope:
        args += list(tables)
        in_specs += [pl.BlockSpec((bm, HEAD_DIM), lambda j, i: (i % blocks_per_seq, 0))] * 2
    if perm_out:
        args.append(perm_mat)
        in_specs.append(_const_spec(perm_mat.shape, 2))
    out_shape = [jax.ShapeDtypeStruct((n, n_cols), BF16)]
    out_specs = [pl.BlockSpec((bm, bn), lambda j, i: (i, j))]
    if perm_out:
        out_shape.append(jax.ShapeDtypeStruct((n // seq, N_RES, seq // N_RES, n_cols), BF16))
        out_specs.append(pl.BlockSpec(
            (None, N_RES, bm // N_RES, bn),
            lambda j, i: (i // blocks_per_seq, 0, i % blocks_per_seq, j)))
    res = pl.pallas_call(
        functools.partial(_proj_kernel, bm=bm, bn=bn, rope=rope, perm_out=perm_out),
        grid=(n_cols // bn, n // bm),
        in_specs=in_specs,
        out_specs=out_specs,
        out_shape=out_shape,
        scratch_shapes=[pltpu.VMEM((k, bn), BF16)],
        compiler_params=_cparams(2),
        name=name,
    )(*args)
    return res if perm_out else res[0]


def _wcomb_kernel(wu_ref, wg_ref, grp_ref, scale_ref, wc_ref, wgbf_ref):
    grp = (grp_ref[...] * scale_ref[...]).astype(BF16)
    wc_ref[...] = jnp.dot(wu_ref[...].astype(BF16), grp, preferred_element_type=F32).astype(BF16)
    wgbf_ref[...] = wg_ref[...].astype(BF16)


def _wcomb_call(w_in, w_grp, scale, *, layer, name, bk=1024):
    n_layers, k, e2 = w_in.shape
    n_grp, gc = w_grp.shape[1], w_grp.shape[2]
    e = e2 // 2
    w_shape = jax.ShapeDtypeStruct((k, e), BF16)
    return pl.pallas_call(
        _wcomb_kernel,
        grid=(n_grp, k // bk),
        in_specs=[pl.BlockSpec((None, bk, gc), lambda g, i: (layer, i, g)),
                  pl.BlockSpec((None, bk, gc), lambda g, i: (layer, i, n_grp + g)),
                  pl.BlockSpec((None, None, gc, gc), lambda g, i: (layer, g, 0, 0)),
                  pl.BlockSpec((None, 1, gc), lambda g, i: (layer * n_grp + g, 0, 0))],
        out_specs=[pl.BlockSpec((bk, gc), lambda g, i: (i, g))] * 2,
        out_shape=[w_shape, w_shape],
        compiler_params=_cparams(2),
        name=name,
    )(w_in, w_in, w_grp, scale.reshape(n_layers * n_grp, 1, gc))


def _ah_kernel(hdn_ref, wc_ref, wg_ref, *rest, bm, blocks_per_seq, normed):
    gain_ref = rest[0] if normed else None
    h_ref, carry_ref = rest[-2:]
    blk = pl.program_id(0) % blocks_per_seq
    gc = h_ref.shape[1] // len(POOL_WINDOWS)

    @pl.when(blk == 0)
    def _():
        carry_ref[...] = jnp.zeros_like(carry_ref)

    if gain_ref is None:
        hdn = hdn_ref[...]
    else:
        xf = hdn_ref[...]
        inv = lax.rsqrt(jnp.mean(xf * xf, axis=-1, keepdims=True) + RMS_EPS)
        hdn = ((xf * inv) * gain_ref[...]).astype(BF16)
    pos = blk * bm + lax.broadcasted_iota(jnp.int32, (bm, 1), 0)

    def finish(g, acc_y, acc_g):
        w = POOL_WINDOWS[g]
        cols = slice(g * gc, (g + 1) * gc)
        ext = jnp.concatenate([carry_ref[:, cols], acc_y], axis=0)
        carry_ref[:, cols] = acc_y[bm - POOL_HALO:, :]
        wsum, have = ext, 1
        while have < w:
            wsum = wsum + pltpu.roll(wsum, have, 0)
            have *= 2
        inv_cnt = 1.0 / jnp.minimum(pos + 1, w).astype(F32)
        pooled = wsum[POOL_HALO:, :] * inv_cnt - acc_y
        h_ref[:, cols] = (pooled * (acc_g / (1.0 + jnp.exp(-acc_g)))).astype(h_ref.dtype)

    for g in reversed(range(len(POOL_WINDOWS))):
        cols = slice(g * gc, (g + 1) * gc)
        finish(g, jnp.dot(hdn, wc_ref[:, cols], preferred_element_type=F32),
               jnp.dot(hdn, wg_ref[:, cols], preferred_element_type=F32))


def _ah_call(hdn, w_comb, w_gate, *, seq, name, gain=None, bm=1024):
    n, k = hdn.shape
    e = w_comb.shape[1]
    args = [hdn, w_comb, w_gate]
    in_specs = [pl.BlockSpec((bm, k), lambda i: (i, 0)),
                _const_spec(w_comb.shape, 1),
                _const_spec(w_gate.shape, 1)]
    if gain is not None:
        args.append(gain.reshape(1, k))
        in_specs.append(_const_spec((1, k), 1))
    kern = functools.partial(_ah_kernel, bm=bm, blocks_per_seq=seq // bm, normed=gain is not None)
    return pl.pallas_call(
        kern,
        grid=(n // bm,),
        in_specs=in_specs,
        out_specs=pl.BlockSpec((bm, e), lambda i: (i, 0)),
        out_shape=jax.ShapeDtypeStruct((n, e), BF16),
        scratch_shapes=[pltpu.VMEM((POOL_HALO, e), F32)],
        compiler_params=_cparams(1),
        name=name,
    )(*args)


def _bout_kernel(hg_ref, x_ref, wout_ref, gains_ref, pres_ref, pblk_ref, *rest,
                 bm, res_in, emit_x, norm_specs):
    outs = rest
    steps = RES_ROWS // N_RES
    n_sub = bm // RES_ROWS
    hs = []
    for sub in range(n_sub):
        if not res_in:
            hs.append(hg_ref[sub * RES_ROWS:(sub + 1) * RES_ROWS, :])
            continue
        blk = jnp.concatenate(
            [hg_ref[r, sub * steps:(sub + 1) * steps, :] for r in range(N_RES)], axis=0)
        hs.append(jnp.dot(pres_ref[...], blk, preferred_element_type=F32).astype(BF16))
    xns = []
    for sub in range(n_sub):
        rows = slice(sub * RES_ROWS, (sub + 1) * RES_ROWS)
        xns.append(x_ref[rows, :] + jnp.dot(hs[sub], wout_ref[...], preferred_element_type=F32))
    for sub in range(n_sub):
        rows = slice(sub * RES_ROWS, (sub + 1) * RES_ROWS)
        if emit_x:
            outs[0][rows, :] = xns[sub]
        _emit_norms(xns[sub], gains_ref, pres_ref, pblk_ref, outs[1:] if emit_x else outs,
                    norm_specs, sub * RES_ROWS)


def _bout_call(hg, x, w_out, gains, pres, pblk, *, layer, seq, emit_x, norm_specs, name, bm=512):
    n, d = x.shape
    e = hg.shape[-1]
    blocks_per_seq = seq // bm
    res_in = hg.ndim == 4
    norm_shapes, norm_out_specs = zip(*[_norm_out(n, d, seq, bm, o, dt) for o, dt in norm_specs])
    out_specs, out_shape = list(norm_out_specs), list(norm_shapes)
    if emit_x:
        out_specs.insert(0, pl.BlockSpec((bm, d), lambda i: (i, 0)))
        out_shape.insert(0, jax.ShapeDtypeStruct((n, d), F32))
    kern = functools.partial(_bout_kernel, bm=bm, res_in=res_in, emit_x=emit_x,
                             norm_specs=tuple(norm_specs))
    if res_in:
        hg_spec = pl.BlockSpec((None, N_RES, bm // N_RES, e),
                               lambda i: (i // blocks_per_seq, 0, i % blocks_per_seq, 0))
    else:
        hg_spec = pl.BlockSpec((bm, e), lambda i: (i, 0))
    return pl.pallas_call(
        kern,
        grid=(n // bm,),
        in_specs=[hg_spec,
                  pl.BlockSpec((bm, d), lambda i: (i, 0)),
                  _const_spec(w_out.shape, 1, layer),
                  _const_spec(gains.shape, 1),
                  _const_spec(pres.shape, 1),
                  _const_spec(pblk.shape, 1)],
        out_specs=out_specs,
        out_shape=out_shape,
        compiler_params=_cparams(1),
        name=name,
    )(hg, x, w_out, gains, pres, pblk)


def _band_blocks(blocks):
    nt = (((1,), (1,)), ((), ()))
    ss = [lax.dot_general(q, k, nt, preferred_element_type=F32) for q, k, _, _ in blocks]
    ss = [jnp.where(blk[3], s, NEG_INF) for blk, s in zip(blocks, ss)]
    ms = [jnp.max(s, axis=-1, keepdims=True) for s in ss]
    ps = [jnp.exp2(s - m).astype(BF16) for s, m in zip(ss, ms)]
    pvs = [jnp.dot(p, jnp.concatenate([blk[2], jnp.ones_like(blk[2])], axis=1),
                   preferred_element_type=F32) for blk, p in zip(blocks, ps)]
    return [(pv[:, :HEAD_DIM], m, pv[:, HEAD_DIM:]) for pv, m in zip(pvs, ms)]


def _band_masks(row_idx, col_idx, row_idx1, col_idx1):
    band = (col_idx >= row_idx) & (col_idx <= row_idx + BAND)
    return band, col_idx1 <= row_idx1


def _attn_kernel(q0_ref, kn_ref, vn_ref, q1_ref, q2_ref, gate_ref, kp_ref, vp_ref, out_ref,
                 a0_s, m0_s, l0_s):
    seq = q0_ref.shape[0]
    steps = seq // N_RES
    row = lax.broadcasted_iota(jnp.int32, (BAND, 2 * BAND), 0)
    col = lax.broadcasted_iota(jnp.int32, (BAND, 2 * BAND), 1)
    row1 = lax.broadcasted_iota(jnp.int32, (BAND, BAND), 0)
    col1 = lax.broadcasted_iota(jnp.int32, (BAND, BAND), 1)
    mask_band, mask_first = _band_masks(row, col, row1, col1)
    blk_pos = lambda p: N_RES * (p % SUBLANES) + p // SUBLANES
    mask0_band, mask0_first = _band_masks(blk_pos(row), col, blk_pos(row1), col1)
    d4 = DILATIONS[1]
    chunk = BAND // d4
    n_sub = N_RES // d4
    d4_pos = lambda p: d4 * (p % chunk) + p // chunk
    mask4_band, mask4_first = _band_masks(
        d4_pos(row), d4_pos(col % BAND) + BAND * (col // BAND), d4_pos(row1), d4_pos(col1))

    def bcast(m):
        return jnp.broadcast_to(m, (m.shape[0], LANES))

    unroll0 = 16
    span = unroll0 * BAND
    tile = (N_RES, SUBLANES, LANES)

    def g0_group(it, first):
        base = 0 if first else pl.multiple_of(it * span, span)
        q_all = q0_ref[pl.ds(base, span), :]
        if first:
            k_all, v_all = kn_ref[0:span, :], vn_ref[0:span, :]
        else:
            lo = pl.multiple_of(base - BAND, BAND)
            k_all, v_all = kn_ref[pl.ds(lo, span + BAND), :], vn_ref[pl.ds(lo, span + BAND), :]
        blocks = []
        for u in range(unroll0):
            q = q_all[u * BAND:(u + 1) * BAND, :]
            if first and u == 0:
                blocks.append((q, k_all[0:BAND, :], v_all[0:BAND, :], mask0_first))
            else:
                k0 = (u - 1) * BAND if first else u * BAND
                blocks.append((q, k_all[k0:k0 + 2 * BAND, :], v_all[k0:k0 + 2 * BAND, :], mask0_band))
        for u, (acc, m, l) in enumerate(_band_blocks(blocks)):
            a0_s[it * unroll0 + u] = acc.reshape(tile)
            m0_s[it * unroll0 + u] = bcast(m).reshape(tile)
            l0_s[it * unroll0 + u] = l.reshape(tile)

    g0_group(0, True)

    def g0_body(it, carry):
        g0_group(it, False)
        return carry

    lax.fori_loop(1, seq // span, g0_body, 0)

    n_blk2 = steps // BAND
    n_blk1 = steps // chunk

    def class_blocks(r4):
        def rows_of(ref, a):
            return [ref[r4 + d4 * k, a * chunk:(a + 1) * chunk, :] for k in range(n_sub)]

        blocks = []
        for k in range(n_sub):
            r = r4 + d4 * k
            q_all, k_all, v_all = q2_ref[r], kp_ref[r], vp_ref[r]
            for nb in range(n_blk2):
                q = q_all[nb * BAND:(nb + 1) * BAND, :]
                if nb == 0:
                    blocks.append((q, k_all[0:BAND, :], v_all[0:BAND, :], mask_first))
                else:
                    k0 = (nb - 1) * BAND
                    blocks.append((q, k_all[k0:k0 + 2 * BAND, :], v_all[k0:k0 + 2 * BAND, :], mask_band))
        for a in range(n_blk1):
            q = jnp.concatenate(rows_of(q1_ref, a), axis=0)
            if a == 0:
                blocks.append((q, jnp.concatenate(rows_of(kp_ref, 0), axis=0),
                               jnp.concatenate(rows_of(vp_ref, 0), axis=0), mask4_first))
            else:
                blocks.append((q,
                               jnp.concatenate(rows_of(kp_ref, a - 1) + rows_of(kp_ref, a), axis=0),
                               jnp.concatenate(rows_of(vp_ref, a - 1) + rows_of(vp_ref, a), axis=0),
                               mask4_band))
        return _band_blocks(blocks)

    def class_merge(r4, res):
        res2, res1 = res[:n_sub * n_blk2], res[n_sub * n_blk2:]
        for k in range(n_sub):
            r = r4 + d4 * k
            rows1 = slice(k * chunk, (k + 1) * chunk)
            for nb in range(n_blk2):
                rows = slice(nb * BAND, (nb + 1) * BAND)
                blk0 = slice(nb * BAND // SUBLANES, (nb + 1) * BAND // SUBLANES)
                a0 = a0_s[blk0, r].reshape(BAND, LANES)
                m0 = m0_s[blk0, r].reshape(BAND, LANES)
                l0 = l0_s[blk0, r].reshape(BAND, LANES)
                part1 = res1[nb * BAND // chunk:(nb + 1) * BAND // chunk]
                a1 = jnp.concatenate([acc[rows1, :] for acc, _, _ in part1], axis=0)
                m1 = jnp.concatenate([bcast(m[rows1, :]) for _, m, _ in part1], axis=0)
                l1 = jnp.concatenate([l[rows1, :] for _, _, l in part1], axis=0)
                a2, m2, l2 = res2[k * n_blk2 + nb]
                m2 = bcast(m2)
                mx = jnp.maximum(jnp.maximum(m0, m1), m2)
                w0, w1, w2 = jnp.exp2(m0 - mx), jnp.exp2(m1 - mx), jnp.exp2(m2 - mx)
                num = w0 * a0 + w1 * a1 + w2 * a2
                den = w0 * l0 + w1 * l1 + w2 * l2
                gt = gate_ref[r, rows, :].astype(F32)
                out_ref[r, rows, :] = ((num * gt) / (den * (1.0 + jnp.exp(-gt)))).astype(out_ref.dtype)

    pending = class_blocks(0)
    for r4 in range(d4):
        upcoming = class_blocks(r4 + 1) if r4 + 1 < d4 else None
        class_merge(r4, pending)
        pending = upcoming


def _attn_call(q0, k_nat, v_nat, q12, gate, k_res, v_res, *, n_heads, name):
    b, seq, e = q0.shape
    steps = seq // N_RES
    nat = lambda off: pl.BlockSpec((None, seq, HEAD_DIM), lambda bi, h: (bi, 0, h + off))
    res = lambda off: pl.BlockSpec((None, N_RES, steps, HEAD_DIM), lambda bi, h: (bi, 0, 0, h + off))
    blk_scratch = pltpu.VMEM((seq // BLK_ROWS, N_RES, SUBLANES, LANES), F32)
    return pl.pallas_call(
        _attn_kernel,
        grid=(b, n_heads),
        in_specs=[nat(0), nat(0), nat(0), res(0), res(n_heads), res(0), res(0), res(0)],
        out_specs=res(0),
        out_shape=jax.ShapeDtypeStruct((b, N_RES, steps, e), BF16),
        scratch_shapes=[blk_scratch] * 3,
        compiler_params=_cparams(2),
        name=name,
    )(q0, k_nat, v_nat, q12, q12, gate, k_res, v_res)


def _rope_tables(seq):
    inv_freq = 1.0 / (ROPE_THETA ** (jnp.arange(0, HEAD_DIM, 2, dtype=F32) / HEAD_DIM))
    ang = jnp.arange(seq, dtype=F32)[:, None] * inv_freq[None, :]
    cos, sin = jnp.cos(ang), jnp.sin(ang)
    return jnp.concatenate([cos, cos], axis=-1), jnp.concatenate([-sin, sin], axis=-1)


def _to_res_order(table, seq):
    return table.reshape(seq // N_RES, N_RES, -1).transpose(1, 0, 2).reshape(seq, -1)


def _to_blk_order(table, seq):
    steps = BLK_ROWS // N_RES
    return table.reshape(seq // BLK_ROWS, steps, N_RES, -1).transpose(0, 2, 1, 3).reshape(seq, -1)


def _perm_matrix(n_rows):
    steps = n_rows // N_RES
    idx = np.arange(n_rows)
    src = (idx % steps) * N_RES + idx // steps
    return jnp.asarray(src[:, None] == idx[None, :], dtype=BF16)


def kernel(x, norm_a, w_in_a, w_grp_a, scale_a, w_out_a, norm_kv, w_k, w_v, norm_b, w_in_b,
           w_out_b, norm_f):
    b, seq, d = x.shape
    n = b * seq
    n_a = w_in_a.shape[0]
    n_b = w_in_b.shape[0]
    e_b = w_k.shape[1]
    n_heads = e_b // HEAD_DIM
    assert seq % (N_RES * BAND) == 0 and d % HEAD_DIM == 0

    rope_nat = _rope_tables(seq)
    rope_res = tuple(_to_res_order(t, seq) * QK_SCALE for t in rope_nat)
    rope_blk = tuple(_to_blk_order(t, seq) * QK_SCALE for t in rope_nat)
    pres, pblk = _perm_matrix(RES_ROWS), _perm_matrix(BLK_ROWS)

    xf = x.reshape(n, d)
    hdn = None

    w_out_a_bf, w_out_b_bf = w_out_a.astype(BF16), w_out_b.astype(BF16)
    for i in range(n_a):
        w_comb, w_gate = _wcomb_call(w_in_a, w_grp_a, scale_a, layer=i, name=f"a{i}_wcomb")
        if hdn is None:
            h = _ah_call(xf, w_comb, w_gate, seq=seq, gain=norm_a[i], name=f"a{i}_h")
        else:
            h = _ah_call(hdn, w_comb, w_gate, seq=seq, name=f"a{i}_h")
        if i < n_a - 1:
            gains = norm_a[i + 1][None]
            specs = [("nat", BF16)]
        else:
            gains = jnp.stack([norm_kv, norm_b[0], norm_b[0]])
            specs = [("nat", BF16), ("blk", BF16), ("res", BF16)]
        outs = _bout_call(h, xf, w_out_a_bf, gains, pres, pblk, layer=i, seq=seq, emit_x=True,
                          norm_specs=specs, name=f"a{i}_out")
        xf = outs[0]
        if i < n_a - 1:
            hdn = outs[1]
        else:
            hdn_kv, hdn_blk, hdn_res = outs[1], outs[2], outs[3]

    k_nat, k_res = _proj_call(hdn_kv, w_k[None], seq=seq, col_off=0, n_cols=e_b,
                              tables=rope_nat, perm_mat=pres, name="k")
    v_nat, v_res = _proj_call(hdn_kv, w_v[None], seq=seq, col_off=0, n_cols=e_b,
                              perm_mat=pres, name="v")
    k_nat, v_nat = k_nat.reshape(b, seq, e_b), v_nat.reshape(b, seq, e_b)

    out = None
    res_shape = lambda c: (b, N_RES, seq // N_RES, c)
    for i in range(n_b):
        hdn_res2 = hdn_res.reshape(n, d)
        q0 = _proj_call(hdn_blk, w_in_b, layer=i, seq=seq, col_off=0, n_cols=e_b, tables=rope_blk,
                        name=f"b{i}_q0")
        q12 = _proj_call(hdn_res2, w_in_b, layer=i, seq=seq, col_off=e_b, n_cols=2 * e_b,
                         tables=rope_res, name=f"b{i}_q12")
        gate = _proj_call(hdn_res2, w_in_b, layer=i, seq=seq, col_off=3 * e_b, n_cols=e_b,
                          name=f"b{i}_gate")
        hg = _attn_call(q0.reshape(b, seq, e_b), k_nat, v_nat, q12.reshape(res_shape(2 * e_b)),
                        gate.reshape(res_shape(e_b)), k_res, v_res, n_heads=n_heads,
                        name=f"b{i}_attn")
        if i < n_b - 1:
            gains = jnp.stack([norm_b[i + 1], norm_b[i + 1]])
            xf, hdn_blk, hdn_res = _bout_call(
                hg, xf, w_out_b_bf, gains, pres, pblk, layer=i, seq=seq, emit_x=True,
                norm_specs=[("blk", BF16), ("res", BF16)], name=f"b{i}_out")
        else:
            (out,) = _bout_call(hg, xf, w_out_b_bf, norm_f[None], pres, pblk, layer=i, seq=seq,
                                emit_x=False, norm_specs=[("nat", F32)], name=f"b{i}_out")
    return out.reshape(b, seq, d)
```

```python
import functools
import math

import jax
import jax.numpy as jnp
import numpy as np
from jax import lax
from jax.experimental import pallas as pl
from jax.experimental.pallas import tpu as pltpu

F32 = jnp.float32
BF16 = jnp.bfloat16

RMS_EPS = 1e-6
POOL_WINDOWS = (2, 4, 8, 16)
POOL_HALO = 16
HEAD_DIM = 128
ROPE_THETA = 10000.0
NEG_INF = -1e30
N_RES = 16
DILATIONS = (1, 4, 16)
BAND = 128
SUBLANES = 8
LANES = 128
RES_ROWS = N_RES * N_RES
SUB_ROWS = RES_ROWS
BLK_ROWS = BAND
VMEM_LIMIT = 56 * 1024 * 1024
QK_SCALE = math.log2(math.e) / math.sqrt(HEAD_DIM)


def _cparams(n_axes):
    return pltpu.CompilerParams(
        dimension_semantics=("arbitrary",) * n_axes, vmem_limit_bytes=VMEM_LIMIT)


def _const_spec(shape, n_grid, layer=None):
    if layer is None:
        block, index = shape, (0,) * len(shape)
    else:
        block, index = (None,) + tuple(shape[1:]), (layer,) + (0,) * (len(shape) - 1)
    if n_grid == 1:
        return pl.BlockSpec(block, lambda i: index, pipeline_mode=pl.Buffered(1))
    return pl.BlockSpec(block, lambda j, i: index, pipeline_mode=pl.Buffered(1))


def _store_res(dst_ref, p_ref, rows, row0=0):
    steps = RES_ROWS // N_RES
    for sub in range(rows.shape[0] // RES_ROWS):
        pb = jnp.dot(p_ref[...], rows[sub * RES_ROWS:(sub + 1) * RES_ROWS, :],
                     preferred_element_type=F32).astype(dst_ref.dtype)
        s0 = row0 // N_RES + sub * steps
        for r in range(N_RES):
            dst_ref[r, s0:s0 + steps, :] = pb[r * steps:(r + 1) * steps, :]


def _emit_norms(xn, gains_ref, pres_ref, pblk_ref, out_refs, norm_specs, row0):
    inv = lax.rsqrt(jnp.mean(xn * xn, axis=-1, keepdims=True) + RMS_EPS)
    xh = xn * inv
    for k, (order, dtype) in enumerate(norm_specs):
        hd = (xh * gains_ref[k:k + 1, :]).astype(dtype)
        if order == "nat":
            out_refs[k][row0:row0 + xn.shape[0], :] = hd
        elif order == "res":
            _store_res(out_refs[k], pres_ref, hd, row0)
        else:
            for sub in range(xn.shape[0] // BLK_ROWS):
                src = slice(sub * BLK_ROWS, (sub + 1) * BLK_ROWS)
                dst = slice(row0 + sub * BLK_ROWS, row0 + (sub + 1) * BLK_ROWS)
                out_refs[k][dst, :] = jnp.dot(pblk_ref[...], hd[src, :],
                                              preferred_element_type=F32).astype(dtype)


def _norm_out(n, d, seq, bm, order, dtype):
    blocks_per_seq = seq // bm
    if order == "res":
        shape = jax.ShapeDtypeStruct((n // seq, N_RES, seq // N_RES, d), dtype)
        spec = pl.BlockSpec((None, N_RES, bm // N_RES, d),
                            lambda i: (i // blocks_per_seq, 0, i % blocks_per_seq, 0))
    else:
        shape = jax.ShapeDtypeStruct((n, d), dtype)
        spec = pl.BlockSpec((bm, d), lambda i: (i, 0))
    return shape, spec


def _proj_kernel(*refs, bm, bn, rope, perm_out):
    refs = list(refs)
    lhs_ref, w_ref = refs[:2]
    del refs[:2]
    if rope:
        cos_ref, sin_ref = refs[:2]
        del refs[:2]
    if perm_out:
        p_ref = refs.pop(0)
    o_ref = refs.pop(0)
    wbf_ref = refs.pop()

    @pl.when(pl.program_id(1) == 0)
    def _():
        wbf_ref[...] = w_ref[...].astype(BF16)

    sub_rows = bm if perm_out else SUB_ROWS
    for r0 in range(0, bm, sub_rows):
        rows = slice(r0, r0 + sub_rows)
        acc = jnp.dot(lhs_ref[rows, :], wbf_ref[...], preferred_element_type=F32)
        if rope:
            cos = cos_ref[rows, :]
            sin = sin_ref[rows, :]
            heads = []
            for hh in range(bn // HEAD_DIM):
                t = acc[:, hh * HEAD_DIM:(hh + 1) * HEAD_DIM]
                rot = pltpu.roll(t, HEAD_DIM // 2, 1)
                heads.append((t * cos + rot * sin).astype(o_ref.dtype))
            res = jnp.concatenate(heads, axis=1)
        else:
            res = acc.astype(o_ref.dtype)
        o_ref[rows, :] = res
        if perm_out:
            _store_res(refs[0], p_ref, res, r0)


def _proj_call(lhs, w, *, seq, col_off, n_cols, name, layer=0, tables=None, perm_mat=None,
               bm=1024, bn=1024):
    n, k = lhs.shape
    blocks_per_seq = seq // bm
    col_blk0 = col_off // bn
    rope, perm_out = tables is not None, perm_mat is not None
    args = [lhs, w]
    in_specs = [pl.BlockSpec((bm, k), lambda j, i: (i, 0)),
                pl.BlockSpec((None, k, bn), lambda j, i: (layer, 0, j + col_blk0))]
    if rope:
        stacked, i_cos, i_sin = tables
        args += [stacked, stacked]
        in_specs += [pl.BlockSpec((None, bm, HEAD_DIM), lambda j, i: (i_cos, i % blocks_per_seq, 0)),
                     pl.BlockSpec((None, bm, HEAD_DIM), lambda j, i: (i_sin, i % blocks_per_seq, 0))]
    if perm_out:
        args.append(perm_mat)
        in_specs.append(_const_spec(perm_mat.shape, 2))
    out_shape = [jax.ShapeDtypeStruct((n, n_cols), BF16)]
    out_specs = [pl.BlockSpec((bm, bn), lambda j, i: (i, j))]
    if perm_out:
        out_shape.append(jax.ShapeDtypeStruct((n // seq, N_RES, seq // N_RES, n_cols), BF16))
        out_specs.append(pl.BlockSpec(
            (None, N_RES, bm // N_RES, bn),
            lambda j, i: (i // blocks_per_seq, 0, i % blocks_per_seq, j)))
    res = pl.pallas_call(
        functools.partial(_proj_kernel, bm=bm, bn=bn, rope=rope, perm_out=perm_out),
        grid=(n_cols // bn, n // bm),
        in_specs=in_specs,
        out_specs=out_specs,
        out_shape=out_shape,
        scratch_shapes=[pltpu.VMEM((k, bn), BF16)],
        compiler_params=_cparams(2),
        name=name,
    )(*args)
    return res if perm_out else res[0]


def _wcomb_kernel(wu_ref, wg_ref, wo_ref, grp_ref, scale_ref, wc_ref, wgbf_ref, wobf_ref):
    grp = (grp_ref[...] * scale_ref[...]).astype(BF16)
    wc_ref[...] = jnp.dot(wu_ref[...].astype(BF16), grp, preferred_element_type=F32).astype(BF16)
    wgbf_ref[...] = wg_ref[...].astype(BF16)
    wobf_ref[...] = wo_ref[...].astype(BF16)


def _wcomb_call(w_in, w_grp, scale, w_out, *, layer, name, bk=1024):
    n_layers, k, e2 = w_in.shape
    n_grp, gc = w_grp.shape[1], w_grp.shape[2]
    e = e2 // 2
    assert w_out.shape[1:] == (e, k) and e == k
    w_shape = jax.ShapeDtypeStruct((k, e), BF16)
    return pl.pallas_call(
        _wcomb_kernel,
        grid=(n_grp, k // bk),
        in_specs=[pl.BlockSpec((None, bk, gc), lambda g, i: (layer, i, g)),
                  pl.BlockSpec((None, bk, gc), lambda g, i: (layer, i, n_grp + g)),
                  pl.BlockSpec((None, bk, gc), lambda g, i: (layer, i, g)),
                  pl.BlockSpec((None, None, gc, gc), lambda g, i: (layer, g, 0, 0)),
                  pl.BlockSpec((None, 1, gc), lambda g, i: (layer * n_grp + g, 0, 0))],
        out_specs=[pl.BlockSpec((bk, gc), lambda g, i: (i, g))] * 3,
        out_shape=[w_shape, w_shape, w_shape],
        compiler_params=_cparams(2),
        name=name,
    )(w_in, w_in, w_out, w_grp, scale.reshape(n_layers * n_grp, 1, gc))


def _ah_kernel(hdn_ref, wc_ref, wg_ref, *rest, bm, blocks_per_seq, normed):
    gain_ref = rest[0] if normed else None
    h_ref, carry_ref = rest[-2:]
    blk = pl.program_id(0) % blocks_per_seq
    gc = h_ref.shape[1] // len(POOL_WINDOWS)

    @pl.when(blk == 0)
    def _():
        carry_ref[...] = jnp.zeros_like(carry_ref)

    if gain_ref is None:
        hdn = hdn_ref[...]
    else:
        xf = hdn_ref[...]
        inv = lax.rsqrt(jnp.mean(xf * xf, axis=-1, keepdims=True) + RMS_EPS)
        hdn = ((xf * inv) * gain_ref[...]).astype(BF16)
    pos = blk * bm + lax.broadcasted_iota(jnp.int32, (bm, 1), 0)

    def finish(g, acc_y, acc_g):
        w = POOL_WINDOWS[g]
        cols = slice(g * gc, (g + 1) * gc)
        ext = jnp.concatenate([carry_ref[:, cols], acc_y], axis=0)
        carry_ref[:, cols] = acc_y[bm - POOL_HALO:, :]
        wsum, have = ext, 1
        while have < w:
            wsum = wsum + pltpu.roll(wsum, have, 0)
            have *= 2
        inv_cnt = 1.0 / jnp.minimum(pos + 1, w).astype(F32)
        pooled = wsum[POOL_HALO:, :] * inv_cnt - acc_y
        h_ref[:, cols] = (pooled * (acc_g / (1.0 + jnp.exp(-acc_g)))).astype(h_ref.dtype)

    for g in reversed(range(len(POOL_WINDOWS))):
        cols = slice(g * gc, (g + 1) * gc)
        finish(g, jnp.dot(hdn, wc_ref[:, cols], preferred_element_type=F32),
               jnp.dot(hdn, wg_ref[:, cols], preferred_element_type=F32))


def _ah_call(hdn, w_comb, w_gate, *, seq, name, gain=None, bm=1024):
    n, k = hdn.shape
    e = w_comb.shape[1]
    args = [hdn, w_comb, w_gate]
    in_specs = [pl.BlockSpec((bm, k), lambda i: (i, 0)),
                _const_spec(w_comb.shape, 1),
                _const_spec(w_gate.shape, 1)]
    if gain is not None:
        args.append(gain.reshape(1, k))
        in_specs.append(_const_spec((1, k), 1))
    kern = functools.partial(_ah_kernel, bm=bm, blocks_per_seq=seq // bm, normed=gain is not None)
    return pl.pallas_call(
        kern,
        grid=(n // bm,),
        in_specs=in_specs,
        out_specs=pl.BlockSpec((bm, e), lambda i: (i, 0)),
        out_shape=jax.ShapeDtypeStruct((n, e), BF16),
        scratch_shapes=[pltpu.VMEM((POOL_HALO, e), F32)],
        compiler_params=_cparams(1),
        name=name,
    )(*args)


def _bout_kernel(hg_ref, x_ref, wout_ref, gains_ref, pres_ref, pblk_ref, *rest,
                 bm, res_in, emit_x, norm_specs):
    outs = rest
    steps = RES_ROWS // N_RES
    n_sub = bm // RES_ROWS
    hs = []
    for sub in range(n_sub):
        if not res_in:
            hs.append(hg_ref[sub * RES_ROWS:(sub + 1) * RES_ROWS, :])
            continue
        blk = jnp.concatenate(
            [hg_ref[r, sub * steps:(sub + 1) * steps, :] for r in range(N_RES)], axis=0)
        hs.append(jnp.dot(pres_ref[...], blk, preferred_element_type=F32).astype(BF16))
    xns = []
    for sub in range(n_sub):
        rows = slice(sub * RES_ROWS, (sub + 1) * RES_ROWS)
        xns.append(x_ref[rows, :] + jnp.dot(hs[sub], wout_ref[...], preferred_element_type=F32))
    for sub in range(n_sub):
        rows = slice(sub * RES_ROWS, (sub + 1) * RES_ROWS)
        if emit_x:
            outs[0][rows, :] = xns[sub]
        _emit_norms(xns[sub], gains_ref, pres_ref, pblk_ref, outs[1:] if emit_x else outs,
                    norm_specs, sub * RES_ROWS)


def _bout_call(hg, x, w_out, gains, pres, pblk, *, layer, seq, emit_x, norm_specs, name, bm=512):
    n, d = x.shape
    e = hg.shape[-1]
    blocks_per_seq = seq // bm
    res_in = hg.ndim == 4
    norm_shapes, norm_out_specs = zip(*[_norm_out(n, d, seq, bm, o, dt) for o, dt in norm_specs])
    out_specs, out_shape = list(norm_out_specs), list(norm_shapes)
    if emit_x:
        out_specs.insert(0, pl.BlockSpec((bm, d), lambda i: (i, 0)))
        out_shape.insert(0, jax.ShapeDtypeStruct((n, d), F32))
    kern = functools.partial(_bout_kernel, bm=bm, res_in=res_in, emit_x=emit_x,
                             norm_specs=tuple(norm_specs))
    if res_in:
        hg_spec = pl.BlockSpec((None, N_RES, bm // N_RES, e),
                               lambda i: (i // blocks_per_seq, 0, i % blocks_per_seq, 0))
    else:
        hg_spec = pl.BlockSpec((bm, e), lambda i: (i, 0))
    return pl.pallas_call(
        kern,
        grid=(n // bm,),
        in_specs=[hg_spec,
                  pl.BlockSpec((bm, d), lambda i: (i, 0)),
                  _const_spec(w_out.shape, 1, layer),
                  _const_spec(gains.shape, 1),
                  _const_spec(pres.shape, 1),
                  _const_spec(pblk.shape, 1)],
        out_specs=out_specs,
        out_shape=out_shape,
        compiler_params=_cparams(1),
        name=name,
    )(hg, x, w_out, gains, pres, pblk)


def _band_blocks(blocks):
    nt = (((1,), (1,)), ((), ()))
    ss = [lax.dot_general(q, k, nt, preferred_element_type=F32) for q, k, _, _ in blocks]
    ss = [jnp.where(blk[3], s, NEG_INF) for blk, s in zip(blocks, ss)]
    ms = [jnp.max(s, axis=-1, keepdims=True) for s in ss]
    ps = [jnp.exp2(s - m).astype(BF16) for s, m in zip(ss, ms)]
    pvs = [jnp.dot(p, jnp.concatenate([blk[2], jnp.ones_like(blk[2])], axis=1),
                   preferred_element_type=F32) for blk, p in zip(blocks, ps)]
    return [(pv[:, :HEAD_DIM], m, pv[:, HEAD_DIM:]) for pv, m in zip(pvs, ms)]


def _band_masks(row_idx, col_idx, row_idx1, col_idx1):
    band = (col_idx >= row_idx) & (col_idx <= row_idx + BAND)
    return band, col_idx1 <= row_idx1


def _attn_kernel(q0_ref, kn_ref, vn_ref, q1_ref, q2_ref, gate_ref, kp_ref, vp_ref, out_ref,
                 a0_s, m0_s, l0_s):
    seq = q0_ref.shape[0]
    steps = seq // N_RES
    row = lax.broadcasted_iota(jnp.int32, (BAND, 2 * BAND), 0)
    col = lax.broadcasted_iota(jnp.int32, (BAND, 2 * BAND), 1)
    row1 = lax.broadcasted_iota(jnp.int32, (BAND, BAND), 0)
    col1 = lax.broadcasted_iota(jnp.int32, (BAND, BAND), 1)
    mask_band, mask_first = _band_masks(row, col, row1, col1)
    blk_pos = lambda p: N_RES * (p % SUBLANES) + p // SUBLANES
    mask0_band, mask0_first = _band_masks(blk_pos(row), col, blk_pos(row1), col1)
    d4 = DILATIONS[1]
    chunk = BAND // d4
    n_sub = N_RES // d4
    d4_pos = lambda p: d4 * (p % chunk) + p // chunk
    mask4_band, mask4_first = _band_masks(
        d4_pos(row), d4_pos(col % BAND) + BAND * (col // BAND), d4_pos(row1), d4_pos(col1))

    def bcast(m):
        return jnp.broadcast_to(m, (m.shape[0], LANES))

    unroll0 = 16
    span = unroll0 * BAND
    tile = (N_RES, SUBLANES, LANES)

    def g0_group(it, first):
        base = 0 if first else pl.multiple_of(it * span, span)
        q_all = q0_ref[pl.ds(base, span), :]
        if first:
            k_all, v_all = kn_ref[0:span, :], vn_ref[0:span, :]
        else:
            lo = pl.multiple_of(base - BAND, BAND)
            k_all, v_all = kn_ref[pl.ds(lo, span + BAND), :], vn_ref[pl.ds(lo, span + BAND), :]
        blocks = []
        for u in range(unroll0):
            q = q_all[u * BAND:(u + 1) * BAND, :]
            if first and u == 0:
                blocks.append((q, k_all[0:BAND, :], v_all[0:BAND, :], mask0_first))
            else:
                k0 = (u - 1) * BAND if first else u * BAND
                blocks.append((q, k_all[k0:k0 + 2 * BAND, :], v_all[k0:k0 + 2 * BAND, :], mask0_band))
        for u, (acc, m, l) in enumerate(_band_blocks(blocks)):
            a0_s[it * unroll0 + u] = acc.reshape(tile)
            m0_s[it * unroll0 + u] = bcast(m).reshape(tile)
            l0_s[it * unroll0 + u] = l.reshape(tile)

    g0_group(0, True)

    def g0_body(it, carry):
        g0_group(it, False)
        return carry

    lax.fori_loop(1, seq // span, g0_body, 0)

    n_blk2 = steps // BAND
    n_blk1 = steps // chunk

    def class_blocks(r4):
        def rows_of(ref, a):
            return [ref[r4 + d4 * k, a * chunk:(a + 1) * chunk, :] for k in range(n_sub)]

        blocks = []
        for k in range(n_sub):
            r = r4 + d4 * k
            q_all, k_all, v_all = q2_ref[r], kp_ref[r], vp_ref[r]
            for nb in range(n_blk2):
                q = q_all[nb * BAND:(nb + 1) * BAND, :]
                if nb == 0:
                    blocks.append((q, k_all[0:BAND, :], v_all[0:BAND, :], mask_first))
                else:
                    k0 = (nb - 1) * BAND
                    blocks.append((q, k_all[k0:k0 + 2 * BAND, :], v_all[k0:k0 + 2 * BAND, :], mask_band))
        for a in range(n_blk1):
            q = jnp.concatenate(rows_of(q1_ref, a), axis=0)
            if a == 0:
                blocks.append((q, jnp.concatenate(rows_of(kp_ref, 0), axis=0),
                               jnp.concatenate(rows_of(vp_ref, 0), axis=0), mask4_first))
            else:
                blocks.append((q,
                               jnp.concatenate(rows_of(kp_ref, a - 1) + rows_of(kp_ref, a), axis=0),
                               jnp.concatenate(rows_of(vp_ref, a - 1) + rows_of(vp_ref, a), axis=0),
                               mask4_band))
        return _band_blocks(blocks)

    def class_merge(r4, res):
        res2, res1 = res[:n_sub * n_blk2], res[n_sub * n_blk2:]
        for k in range(n_sub):
            r = r4 + d4 * k
            rows1 = slice(k * chunk, (k + 1) * chunk)
            for nb in range(n_blk2):
                rows = slice(nb * BAND, (nb + 1) * BAND)
                blk0 = slice(nb * BAND // SUBLANES, (nb + 1) * BAND // SUBLANES)
                a0 = a0_s[blk0, r].reshape(BAND, LANES)
                m0 = m0_s[blk0, r].reshape(BAND, LANES)
                l0 = l0_s[blk0, r].reshape(BAND, LANES)
                part1 = res1[nb * BAND // chunk:(nb + 1) * BAND // chunk]
                a1 = jnp.concatenate([acc[rows1, :] for acc, _, _ in part1], axis=0)
                m1 = jnp.concatenate([bcast(m[rows1, :]) for _, m, _ in part1], axis=0)
                l1 = jnp.concatenate([l[rows1, :] for _, _, l in part1], axis=0)
                a2, m2, l2 = res2[k * n_blk2 + nb]
                m2 = bcast(m2)
                mx = jnp.maximum(jnp.maximum(m0, m1), m2)
                w0, w1, w2 = jnp.exp2(m0 - mx), jnp.exp2(m1 - mx), jnp.exp2(m2 - mx)
                num = w0 * a0 + w1 * a1 + w2 * a2
                den = w0 * l0 + w1 * l1 + w2 * l2
                gt = gate_ref[r, rows, :].astype(F32)
                out_ref[r, rows, :] = ((num * gt) / (den * (1.0 + jnp.exp(-gt)))).astype(out_ref.dtype)

    pending = class_blocks(0)
    for r4 in range(d4):
        upcoming = class_blocks(r4 + 1) if r4 + 1 < d4 else None
        class_merge(r4, pending)
        pending = upcoming


def _attn_call(q0, k_nat, v_nat, q12, gate, k_res, v_res, *, n_heads, name):
    b, seq, e = q0.shape
    steps = seq // N_RES
    nat = lambda off: pl.BlockSpec((None, seq, HEAD_DIM), lambda bi, h: (bi, 0, h + off))
    res = lambda off: pl.BlockSpec((None, N_RES, steps, HEAD_DIM), lambda bi, h: (bi, 0, 0, h + off))
    blk_scratch = pltpu.VMEM((seq // BLK_ROWS, N_RES, SUBLANES, LANES), F32)
    return pl.pallas_call(
        _attn_kernel,
        grid=(b, n_heads),
        in_specs=[nat(0), nat(0), nat(0), res(0), res(n_heads), res(0), res(0), res(0)],
        out_specs=res(0),
        out_shape=jax.ShapeDtypeStruct((b, N_RES, steps, e), BF16),
        scratch_shapes=[blk_scratch] * 3,
        compiler_params=_cparams(2),
        name=name,
    )(q0, k_nat, v_nat, q12, q12, gate, k_res, v_res)


def _rope_tables(seq):
    inv_freq = 1.0 / (ROPE_THETA ** (jnp.arange(0, HEAD_DIM, 2, dtype=F32) / HEAD_DIM))
    ang = jnp.arange(seq, dtype=F32)[:, None] * inv_freq[None, :]
    cos, sin = jnp.cos(ang), jnp.sin(ang)
    return jnp.concatenate([cos, cos], axis=-1), jnp.concatenate([-sin, sin], axis=-1)


def _to_res_order(table, seq):
    return table.reshape(seq // N_RES, N_RES, -1).transpose(1, 0, 2).reshape(seq, -1)


def _to_blk_order(table, seq):
    steps = BLK_ROWS // N_RES
    return table.reshape(seq // BLK_ROWS, steps, N_RES, -1).transpose(0, 2, 1, 3).reshape(seq, -1)


def _perm_matrix(n_rows):
    steps = n_rows // N_RES
    idx = np.arange(n_rows)
    src = (idx % steps) * N_RES + idx // steps
    return jnp.asarray(src[:, None] == idx[None, :], dtype=BF16)


def kernel(x, norm_a, w_in_a, w_grp_a, scale_a, w_out_a, norm_kv, w_k, w_v, norm_b, w_in_b,
           w_out_b, norm_f):
    b, seq, d = x.shape
    n = b * seq
    n_a = w_in_a.shape[0]
    n_b = w_in_b.shape[0]
    e_b = w_k.shape[1]
    n_heads = e_b // HEAD_DIM
    assert seq % (N_RES * BAND) == 0 and d % HEAD_DIM == 0

    cos_nat, sin_nat = _rope_tables(seq)
    tables = jnp.stack(
        [cos_nat, sin_nat]
        + [_to_res_order(t, seq) * QK_SCALE for t in (cos_nat, sin_nat)]
        + [_to_blk_order(t, seq) * QK_SCALE for t in (cos_nat, sin_nat)])
    rope_nat, rope_res, rope_blk = (tables, 0, 1), (tables, 2, 3), (tables, 4, 5)
    pres, pblk = _perm_matrix(RES_ROWS), _perm_matrix(BLK_ROWS)

    xf = x.reshape(n, d)
    hdn = None

    w_out_b_bf = w_out_b.astype(BF16)
    for i in range(n_a):
        w_comb, w_gate, w_out_bf = _wcomb_call(w_in_a, w_grp_a, scale_a, w_out_a, layer=i,
                                               name=f"a{i}_wcomb")
        if hdn is None:
            h = _ah_call(xf, w_comb, w_gate, seq=seq, gain=norm_a[i], name=f"a{i}_h")
        else:
            h = _ah_call(hdn, w_comb, w_gate, seq=seq, name=f"a{i}_h")
        if i < n_a - 1:
            gains = norm_a[i + 1][None]
            specs = [("nat", BF16)]
        else:
            gains = jnp.stack([norm_kv, norm_b[0], norm_b[0]])
            specs = [("nat", BF16), ("blk", BF16), ("res", BF16)]
        outs = _bout_call(h, xf, w_out_bf, gains, pres, pblk, layer=None, seq=seq, emit_x=True,
                          norm_specs=specs, name=f"a{i}_out")
        xf = outs[0]
        if i < n_a - 1:
            hdn = outs[1]
        else:
            hdn_kv, hdn_blk, hdn_res = outs[1], outs[2], outs[3]

    k_nat, k_res = _proj_call(hdn_kv, w_k[None], seq=seq, col_off=0, n_cols=e_b,
                              tables=rope_nat, perm_mat=pres, name="k")
    v_nat, v_res = _proj_call(hdn_kv, w_v[None], seq=seq, col_off=0, n_cols=e_b,
                              perm_mat=pres, name="v")
    k_nat, v_nat = k_nat.reshape(b, seq, e_b), v_nat.reshape(b, seq, e_b)

    out = None
    res_shape = lambda c: (b, N_RES, seq // N_RES, c)
    for i in range(n_b):
        hdn_res2 = hdn_res.reshape(n, d)
        q0 = _proj_call(hdn_blk, w_in_b, layer=i, seq=seq, col_off=0, n_cols=e_b, tables=rope_blk,
                        name=f"b{i}_q0")
        q12 = _proj_call(hdn_res2, w_in_b, layer=i, seq=seq, col_off=e_b, n_cols=2 * e_b,
                         tables=rope_res, name=f"b{i}_q12")
        gate = _proj_call(hdn_res2, w_in_b, layer=i, seq=seq, col_off=3 * e_b, n_cols=e_b,
                          name=f"b{i}_gate")
        hg = _attn_call(q0.reshape(b, seq, e_b), k_nat, v_nat, q12.reshape(res_shape(2 * e_b)),
                        gate.reshape(res_shape(e_b)), k_res, v_res, n_heads=n_heads,
                        name=f"b{i}_attn")
        if i < n_b - 1:
            gains = jnp.stack([norm_b[i + 1], norm_b[i + 1]])
            xf, hdn_blk, hdn_res = _bout_call(
                hg, xf, w_out_b_bf, gains, pres, pblk, layer=i, seq=seq, emit_x=True,
                norm_specs=[("blk", BF16), ("res", BF16)], name=f"b{i}_out")
        else:
            (out,) = _bout_call(hg, xf, w_out_b_bf, norm_f[None], pres, pblk, layer=i, seq=seq,
                                emit_x=False, norm_specs=[("nat", F32)], name=f"b{i}_out")
    return out.reshape(b, seq, d)
```

```python
import functools
import math

import jax
import jax.numpy as jnp
import numpy as np
from jax import lax
from jax.experimental import pallas as pl
from jax.experimental.pallas import tpu as pltpu

F32 = jnp.float32
BF16 = jnp.bfloat16

RMS_EPS = 1e-6
POOL_WINDOWS = (2, 4, 8, 16)
POOL_HALO = 16
HEAD_DIM = 128
ROPE_THETA = 10000.0
NEG_INF = -1e30
N_RES = 16
DILATIONS = (1, 4, 16)
BAND = 128
SUBLANES = 8
LANES = 128
RES_ROWS = N_RES * N_RES
SUB_ROWS = RES_ROWS
BLK_ROWS = BAND
VMEM_LIMIT = 56 * 1024 * 1024
QK_SCALE = math.log2(math.e) / math.sqrt(HEAD_DIM)


def _cparams(n_axes):
    return pltpu.CompilerParams(
        dimension_semantics=("arbitrary",) * n_axes, vmem_limit_bytes=VMEM_LIMIT)


def _const_spec(shape, n_grid, layer=None):
    if layer is None:
        block, index = shape, (0,) * len(shape)
    else:
        block, index = (None,) + tuple(shape[1:]), (layer,) + (0,) * (len(shape) - 1)
    if n_grid == 1:
        return pl.BlockSpec(block, lambda i: index, pipeline_mode=pl.Buffered(1))
    return pl.BlockSpec(block, lambda j, i: index, pipeline_mode=pl.Buffered(1))


def _store_res(dst_ref, p_ref, rows, row0=0):
    steps = RES_ROWS // N_RES
    for sub in range(rows.shape[0] // RES_ROWS):
        pb = jnp.dot(p_ref[...], rows[sub * RES_ROWS:(sub + 1) * RES_ROWS, :],
                     preferred_element_type=F32).astype(dst_ref.dtype)
        s0 = row0 // N_RES + sub * steps
        for r in range(N_RES):
            dst_ref[r, s0:s0 + steps, :] = pb[r * steps:(r + 1) * steps, :]


def _emit_norms(xn, gains_ref, pres_ref, pblk_ref, out_refs, norm_specs, row0):
    inv = lax.rsqrt(jnp.mean(xn * xn, axis=-1, keepdims=True) + RMS_EPS)
    xh = xn * inv
    for k, (order, dtype) in enumerate(norm_specs):
        hd = (xh * gains_ref[k:k + 1, :]).astype(dtype)
        if order == "nat":
            out_refs[k][row0:row0 + xn.shape[0], :] = hd
        elif order == "res":
            _store_res(out_refs[k], pres_ref, hd, row0)
        else:
            for sub in range(xn.shape[0] // BLK_ROWS):
                src = slice(sub * BLK_ROWS, (sub + 1) * BLK_ROWS)
                dst = slice(row0 + sub * BLK_ROWS, row0 + (sub + 1) * BLK_ROWS)
                out_refs[k][dst, :] = jnp.dot(pblk_ref[...], hd[src, :],
                                              preferred_element_type=F32).astype(dtype)


def _norm_out(n, d, seq, bm, order, dtype):
    blocks_per_seq = seq // bm
    if order == "res":
        shape = jax.ShapeDtypeStruct((n // seq, N_RES, seq // N_RES, d), dtype)
        spec = pl.BlockSpec((None, N_RES, bm // N_RES, d),
                            lambda i: (i // blocks_per_seq, 0, i % blocks_per_seq, 0))
    else:
        shape = jax.ShapeDtypeStruct((n, d), dtype)
        spec = pl.BlockSpec((bm, d), lambda i: (i, 0))
    return shape, spec


def _proj_kernel(*refs, bm, bn, rope, perm_out):
    refs = list(refs)
    lhs_ref, w_ref = refs[:2]
    del refs[:2]
    if rope:
        cos_ref, sin_ref = refs[:2]
        del refs[:2]
    if perm_out:
        p_ref = refs.pop(0)
    o_ref = refs.pop(0)
    wbf_ref = refs.pop()

    @pl.when(pl.program_id(1) == 0)
    def _():
        wbf_ref[...] = w_ref[...].astype(BF16)

    sub_rows = bm if perm_out else SUB_ROWS
    for r0 in range(0, bm, sub_rows):
        rows = slice(r0, r0 + sub_rows)
        acc = jnp.dot(lhs_ref[rows, :], wbf_ref[...], preferred_element_type=F32)
        if rope:
            cos = cos_ref[rows, :]
            sin = sin_ref[rows, :]
            heads = []
            for hh in range(bn // HEAD_DIM):
                t = acc[:, hh * HEAD_DIM:(hh + 1) * HEAD_DIM]
                rot = pltpu.roll(t, HEAD_DIM // 2, 1)
                heads.append((t * cos + rot * sin).astype(o_ref.dtype))
            res = jnp.concatenate(heads, axis=1)
        else:
            res = acc.astype(o_ref.dtype)
        o_ref[rows, :] = res
        if perm_out:
            _store_res(refs[0], p_ref, res, r0)


def _proj_call(lhs, w, *, seq, col_off, n_cols, name, layer=0, tables=None, perm_mat=None,
               bm=1024, bn=1024):
    n, k = lhs.shape
    blocks_per_seq = seq // bm
    col_blk0 = col_off // bn
    rope, perm_out = tables is not None, perm_mat is not None
    args = [lhs, w]
    in_specs = [pl.BlockSpec((bm, k), lambda j, i: (i, 0)),
                pl.BlockSpec((None, k, bn), lambda j, i: (layer, 0, j + col_blk0))]
    if rope:
        stacked, i_cos, i_sin = tables
        args += [stacked, stacked]
        in_specs += [pl.BlockSpec((None, bm, HEAD_DIM), lambda j, i: (i_cos, i % blocks_per_seq, 0)),
                     pl.BlockSpec((None, bm, HEAD_DIM), lambda j, i: (i_sin, i % blocks_per_seq, 0))]
    if perm_out:
        args.append(perm_mat)
        in_specs.append(_const_spec(perm_mat.shape, 2))
    out_shape = [jax.ShapeDtypeStruct((n, n_cols), BF16)]
    out_specs = [pl.BlockSpec((bm, bn), lambda j, i: (i, j))]
    if perm_out:
        out_shape.append(jax.ShapeDtypeStruct((n // seq, N_RES, seq // N_RES, n_cols), BF16))
        out_specs.append(pl.BlockSpec(
            (None, N_RES, bm // N_RES, bn),
            lambda j, i: (i // blocks_per_seq, 0, i % blocks_per_seq, j)))
    res = pl.pallas_call(
        functools.partial(_proj_kernel, bm=bm, bn=bn, rope=rope, perm_out=perm_out),
        grid=(n_cols // bn, n // bm),
        in_specs=in_specs,
        out_specs=out_specs,
        out_shape=out_shape,
        scratch_shapes=[pltpu.VMEM((k, bn), BF16)],
        compiler_params=_cparams(2),
        name=name,
    )(*args)
    return res if perm_out else res[0]


def _wcomb_kernel(wu_ref, wg_ref, wo_ref, grp_ref, scale_ref, wc_ref, wgbf_ref, wobf_ref):
    grp = (grp_ref[...] * scale_ref[...]).astype(BF16)
    wc_ref[...] = jnp.dot(wu_ref[...].astype(BF16), grp, preferred_element_type=F32).astype(BF16)
    wgbf_ref[...] = wg_ref[...].astype(BF16)
    wobf_ref[...] = wo_ref[...].astype(BF16)


def _wcomb_call(w_in, w_grp, scale, w_out, *, layer, name, bk=1024):
    n_layers, k, e2 = w_in.shape
    n_grp, gc = w_grp.shape[1], w_grp.shape[2]
    e = e2 // 2
    assert w_out.shape[1:] == (e, k) and e == k
    w_shape = jax.ShapeDtypeStruct((k, e), BF16)
    return pl.pallas_call(
        _wcomb_kernel,
        grid=(n_grp, k // bk),
        in_specs=[pl.BlockSpec((None, bk, gc), lambda g, i: (layer, i, g)),
                  pl.BlockSpec((None, bk, gc), lambda g, i: (layer, i, n_grp + g)),
                  pl.BlockSpec((None, bk, gc), lambda g, i: (layer, i, g)),
                  pl.BlockSpec((None, None, gc, gc), lambda g, i: (layer, g, 0, 0)),
                  pl.BlockSpec((None, 1, gc), lambda g, i: (layer * n_grp + g, 0, 0))],
        out_specs=[pl.BlockSpec((bk, gc), lambda g, i: (i, g))] * 3,
        out_shape=[w_shape, w_shape, w_shape],
        compiler_params=_cparams(2),
        name=name,
    )(w_in, w_in, w_out, w_grp, scale.reshape(n_layers * n_grp, 1, gc))


def _ah_kernel(hdn_ref, wc_ref, wg_ref, *rest, bm, blocks_per_seq, normed):
    gain_ref = rest[0] if normed else None
    h_ref, carry_ref = rest[-2:]
    blk = pl.program_id(0) % blocks_per_seq
    gc = h_ref.shape[1] // len(POOL_WINDOWS)

    @pl.when(blk == 0)
    def _():
        carry_ref[...] = jnp.zeros_like(carry_ref)

    if gain_ref is None:
        hdn = hdn_ref[...]
    else:
        xf = hdn_ref[...]
        inv = lax.rsqrt(jnp.mean(xf * xf, axis=-1, keepdims=True) + RMS_EPS)
        hdn = ((xf * inv) * gain_ref[...]).astype(BF16)
    pos = blk * bm + lax.broadcasted_iota(jnp.int32, (bm, 1), 0)

    def finish(g, acc_y, acc_g):
        w = POOL_WINDOWS[g]
        cols = slice(g * gc, (g + 1) * gc)
        ext = jnp.concatenate([carry_ref[:, cols], acc_y], axis=0)
        carry_ref[:, cols] = acc_y[bm - POOL_HALO:, :]
        wsum, have = ext, 1
        while have < w:
            wsum = wsum + pltpu.roll(wsum, have, 0)
            have *= 2
        inv_cnt = 1.0 / jnp.minimum(pos + 1, w).astype(F32)
        pooled = wsum[POOL_HALO:, :] * inv_cnt - acc_y
        h_ref[:, cols] = (pooled * (acc_g / (1.0 + jnp.exp(-acc_g)))).astype(h_ref.dtype)

    for g in reversed(range(len(POOL_WINDOWS))):
        cols = slice(g * gc, (g + 1) * gc)
        finish(g, jnp.dot(hdn, wc_ref[:, cols], preferred_element_type=F32),
               jnp.dot(hdn, wg_ref[:, cols], preferred_element_type=F32))


def _ah_call(hdn, w_comb, w_gate, *, seq, name, gain=None, bm=1024):
    n, k = hdn.shape
    e = w_comb.shape[1]
    args = [hdn, w_comb, w_gate]
    in_specs = [pl.BlockSpec((bm, k), lambda i: (i, 0)),
                _const_spec(w_comb.shape, 1),
                _const_spec(w_gate.shape, 1)]
    if gain is not None:
        args.append(gain.reshape(1, k))
        in_specs.append(_const_spec((1, k), 1))
    kern = functools.partial(_ah_kernel, bm=bm, blocks_per_seq=seq // bm, normed=gain is not None)
    return pl.pallas_call(
        kern,
        grid=(n // bm,),
        in_specs=in_specs,
        out_specs=pl.BlockSpec((bm, e), lambda i: (i, 0)),
        out_shape=jax.ShapeDtypeStruct((n, e), BF16),
        scratch_shapes=[pltpu.VMEM((POOL_HALO, e), F32)],
        compiler_params=_cparams(1),
        name=name,
    )(*args)


def _bout_kernel(hg_ref, x_ref, wout_ref, gains_ref, pres_ref, pblk_ref, *rest,
                 bm, res_in, emit_x, norm_specs):
    outs = rest
    steps = RES_ROWS // N_RES
    n_sub = bm // RES_ROWS
    hs = []
    for sub in range(n_sub):
        if not res_in:
            hs.append(hg_ref[sub * RES_ROWS:(sub + 1) * RES_ROWS, :])
            continue
        blk = jnp.concatenate(
            [hg_ref[r, sub * steps:(sub + 1) * steps, :] for r in range(N_RES)], axis=0)
        hs.append(jnp.dot(pres_ref[...], blk, preferred_element_type=F32).astype(BF16))
    xns = []
    for sub in range(n_sub):
        rows = slice(sub * RES_ROWS, (sub + 1) * RES_ROWS)
        xns.append(x_ref[rows, :] + jnp.dot(hs[sub], wout_ref[...], preferred_element_type=F32))
    for sub in range(n_sub):
        rows = slice(sub * RES_ROWS, (sub + 1) * RES_ROWS)
        if emit_x:
            outs[0][rows, :] = xns[sub]
        _emit_norms(xns[sub], gains_ref, pres_ref, pblk_ref, outs[1:] if emit_x else outs,
                    norm_specs, sub * RES_ROWS)


def _bout_call(hg, x, w_out, gains, pres, pblk, *, layer, seq, emit_x, norm_specs, name, bm=512):
    n, d = x.shape
    e = hg.shape[-1]
    blocks_per_seq = seq // bm
    res_in = hg.ndim == 4
    norm_shapes, norm_out_specs = zip(*[_norm_out(n, d, seq, bm, o, dt) for o, dt in norm_specs])
    out_specs, out_shape = list(norm_out_specs), list(norm_shapes)
    if emit_x:
        out_specs.insert(0, pl.BlockSpec((bm, d), lambda i: (i, 0)))
        out_shape.insert(0, jax.ShapeDtypeStruct((n, d), F32))
    kern = functools.partial(_bout_kernel, bm=bm, res_in=res_in, emit_x=emit_x,
                             norm_specs=tuple(norm_specs))
    if res_in:
        hg_spec = pl.BlockSpec((None, N_RES, bm // N_RES, e),
                               lambda i: (i // blocks_per_seq, 0, i % blocks_per_seq, 0))
    else:
        hg_spec = pl.BlockSpec((bm, e), lambda i: (i, 0))
    return pl.pallas_call(
        kern,
        grid=(n // bm,),
        in_specs=[hg_spec,
                  pl.BlockSpec((bm, d), lambda i: (i, 0)),
                  _const_spec(w_out.shape, 1, layer),
                  _const_spec(gains.shape, 1),
                  _const_spec(pres.shape, 1),
                  _const_spec(pblk.shape, 1)],
        out_specs=out_specs,
        out_shape=out_shape,
        compiler_params=_cparams(1),
        name=name,
    )(hg, x, w_out, gains, pres, pblk)


def _band_blocks(blocks):
    ss, ms = _band_scores(blocks)
    return [(acc, m, l) for (acc, l), m in zip(_band_outputs(blocks, ss, ms), ms)]


def _band_scores(blocks):
    nt = (((1,), (1,)), ((), ()))
    ss = [lax.dot_general(q, k, nt, preferred_element_type=F32) for q, k, _, _ in blocks]
    ss = [jnp.where(blk[3], s, NEG_INF) for blk, s in zip(blocks, ss)]
    return ss, [jnp.max(s, axis=-1, keepdims=True) for s in ss]


def _band_outputs(blocks, ss, shifts):
    ps = [jnp.exp2(s - m).astype(BF16) for s, m in zip(ss, shifts)]
    pvs = [jnp.dot(p, jnp.concatenate([blk[2], jnp.ones_like(blk[2])], axis=1),
                   preferred_element_type=F32) for blk, p in zip(blocks, ps)]
    return [(pv[:, :HEAD_DIM], pv[:, HEAD_DIM:]) for pv in pvs]


def _band_masks(row_idx, col_idx, row_idx1, col_idx1):
    band = (col_idx >= row_idx) & (col_idx <= row_idx + BAND)
    return band, col_idx1 <= row_idx1


def _attn_kernel(q0_ref, kn_ref, vn_ref, q1_ref, q2_ref, gate_ref, kp_ref, vp_ref, out_ref,
                 a0_s, m0_s, l0_s):
    seq = q0_ref.shape[0]
    steps = seq // N_RES
    row = lax.broadcasted_iota(jnp.int32, (BAND, 2 * BAND), 0)
    col = lax.broadcasted_iota(jnp.int32, (BAND, 2 * BAND), 1)
    row1 = lax.broadcasted_iota(jnp.int32, (BAND, BAND), 0)
    col1 = lax.broadcasted_iota(jnp.int32, (BAND, BAND), 1)
    mask_band, mask_first = _band_masks(row, col, row1, col1)
    blk_pos = lambda p: N_RES * (p % SUBLANES) + p // SUBLANES
    mask0_band, mask0_first = _band_masks(blk_pos(row), col, blk_pos(row1), col1)
    d4 = DILATIONS[1]
    chunk = BAND // d4
    n_sub = N_RES // d4
    d4_pos = lambda p: d4 * (p % chunk) + p // chunk
    mask4_band, mask4_first = _band_masks(
        d4_pos(row), d4_pos(col % BAND) + BAND * (col // BAND), d4_pos(row1), d4_pos(col1))

    def bcast(m):
        return jnp.broadcast_to(m, (m.shape[0], LANES))

    unroll0 = 16
    span = unroll0 * BAND
    tile = (N_RES, SUBLANES, LANES)

    def g0_group(it, first):
        base = 0 if first else pl.multiple_of(it * span, span)
        q_all = q0_ref[pl.ds(base, span), :]
        if first:
            k_all, v_all = kn_ref[0:span, :], vn_ref[0:span, :]
        else:
            lo = pl.multiple_of(base - BAND, BAND)
            k_all, v_all = kn_ref[pl.ds(lo, span + BAND), :], vn_ref[pl.ds(lo, span + BAND), :]
        blocks = []
        for u in range(unroll0):
            q = q_all[u * BAND:(u + 1) * BAND, :]
            if first and u == 0:
                blocks.append((q, k_all[0:BAND, :], v_all[0:BAND, :], mask0_first))
            else:
                k0 = (u - 1) * BAND if first else u * BAND
                blocks.append((q, k_all[k0:k0 + 2 * BAND, :], v_all[k0:k0 + 2 * BAND, :], mask0_band))
        for u, (acc, m, l) in enumerate(_band_blocks(blocks)):
            a0_s[it * unroll0 + u] = acc.reshape(tile)
            m0_s[it * unroll0 + u] = bcast(m).reshape(tile)
            l0_s[it * unroll0 + u] = l.reshape(tile)

    g0_group(0, True)

    def g0_body(it, carry):
        g0_group(it, False)
        return carry

    lax.fori_loop(1, seq // span, g0_body, 0)

    n_blk2 = steps // BAND
    n_blk1 = steps // chunk
    per_half = BAND // chunk

    def class_blocks(r4):
        def rows_of(ref, a):
            return [ref[r4 + d4 * k, a * chunk:(a + 1) * chunk, :] for k in range(n_sub)]

        blocks = []
        for k in range(n_sub):
            r = r4 + d4 * k
            q_all, k_all, v_all = q2_ref[r], kp_ref[r], vp_ref[r]
            for nb in range(n_blk2):
                q = q_all[nb * BAND:(nb + 1) * BAND, :]
                if nb == 0:
                    blocks.append((q, k_all[0:BAND, :], v_all[0:BAND, :], mask_first))
                else:
                    k0 = (nb - 1) * BAND
                    blocks.append((q, k_all[k0:k0 + 2 * BAND, :], v_all[k0:k0 + 2 * BAND, :], mask_band))
        for a in range(n_blk1):
            q = jnp.concatenate(rows_of(q1_ref, a), axis=0)
            if a == 0:
                blocks.append((q, jnp.concatenate(rows_of(kp_ref, 0), axis=0),
                               jnp.concatenate(rows_of(vp_ref, 0), axis=0), mask4_first))
            else:
                blocks.append((q,
                               jnp.concatenate(rows_of(kp_ref, a - 1) + rows_of(kp_ref, a), axis=0),
                               jnp.concatenate(rows_of(vp_ref, a - 1) + rows_of(vp_ref, a), axis=0),
                               mask4_band))
        ss, ms = _band_scores(blocks)
        ms2, ms1 = ms[:n_sub * n_blk2], ms[n_sub * n_blk2:]

        m0s, tops = {}, {}
        for k in range(n_sub):
            r = r4 + d4 * k
            rows1 = slice(k * chunk, (k + 1) * chunk)
            for nb in range(n_blk2):
                blk0 = slice(nb * BAND // SUBLANES, (nb + 1) * BAND // SUBLANES)
                m0 = m0_s[blk0, r].reshape(BAND, LANES)
                m1 = jnp.concatenate(
                    [bcast(m[rows1, :]) for m in ms1[nb * per_half:(nb + 1) * per_half]], axis=0)
                m0s[k, nb] = m0
                tops[k, nb] = jnp.maximum(jnp.maximum(m0, m1), bcast(ms2[k * n_blk2 + nb]))
        shifts = [tops[k, nb] for k in range(n_sub) for nb in range(n_blk2)]
        for a in range(n_blk1):
            piece = slice((a % per_half) * chunk, (a % per_half + 1) * chunk)
            shifts.append(jnp.concatenate(
                [tops[k, a // per_half][piece, :] for k in range(n_sub)], axis=0))
        shifts = [sh if s.shape[1] == LANES else jnp.concatenate([sh, sh], axis=1)
                  for sh, s in zip(shifts, ss)]
        return _band_outputs(blocks, ss, shifts), m0s, tops

    def class_merge(r4, state):
        outs, m0s, tops = state
        outs2, outs1 = outs[:n_sub * n_blk2], outs[n_sub * n_blk2:]
        for k in range(n_sub):
            r = r4 + d4 * k
            rows1 = slice(k * chunk, (k + 1) * chunk)
            for nb in range(n_blk2):
                rows = slice(nb * BAND, (nb + 1) * BAND)
                blk0 = slice(nb * BAND // SUBLANES, (nb + 1) * BAND // SUBLANES)
                a0 = a0_s[blk0, r].reshape(BAND, LANES)
                l0 = l0_s[blk0, r].reshape(BAND, LANES)
                part1 = outs1[nb * per_half:(nb + 1) * per_half]
                a1 = jnp.concatenate([acc[rows1, :] for acc, _ in part1], axis=0)
                l1 = jnp.concatenate([l[rows1, :] for _, l in part1], axis=0)
                a2, l2 = outs2[k * n_blk2 + nb]
                c0 = jnp.exp2(m0s[k, nb] - tops[k, nb])
                num = c0 * a0 + a1 + a2
                den = c0 * l0 + l1 + l2
                gt = gate_ref[r, rows, :].astype(F32)
                out_ref[r, rows, :] = ((num * gt) / (den * (1.0 + jnp.exp(-gt)))).astype(out_ref.dtype)

    pending = class_blocks(0)
    for r4 in range(d4):
        upcoming = class_blocks(r4 + 1) if r4 + 1 < d4 else None
        class_merge(r4, pending)
        pending = upcoming


def _attn_call(q0, k_nat, v_nat, q12, gate, k_res, v_res, *, n_heads, name):
    b, seq, e = q0.shape
    steps = seq // N_RES
    nat = lambda off: pl.BlockSpec((None, seq, HEAD_DIM), lambda bi, h: (bi, 0, h + off))
    res = lambda off: pl.BlockSpec((None, N_RES, steps, HEAD_DIM), lambda bi, h: (bi, 0, 0, h + off))
    blk_scratch = pltpu.VMEM((seq // BLK_ROWS, N_RES, SUBLANES, LANES), F32)
    return pl.pallas_call(
        _attn_kernel,
        grid=(b, n_heads),
        in_specs=[nat(0), nat(0), nat(0), res(0), res(n_heads), res(0), res(0), res(0)],
        out_specs=res(0),
        out_shape=jax.ShapeDtypeStruct((b, N_RES, steps, e), BF16),
        scratch_shapes=[blk_scratch] * 3,
        compiler_params=_cparams(2),
        name=name,
    )(q0, k_nat, v_nat, q12, q12, gate, k_res, v_res)


def _rope_tables(seq):
    inv_freq = 1.0 / (ROPE_THETA ** (jnp.arange(0, HEAD_DIM, 2, dtype=F32) / HEAD_DIM))
    ang = jnp.arange(seq, dtype=F32)[:, None] * inv_freq[None, :]
    cos, sin = jnp.cos(ang), jnp.sin(ang)
    return jnp.concatenate([cos, cos], axis=-1), jnp.concatenate([-sin, sin], axis=-1)


def _to_res_order(table, seq):
    return table.reshape(seq // N_RES, N_RES, -1).transpose(1, 0, 2).reshape(seq, -1)


def _to_blk_order(table, seq):
    steps = BLK_ROWS // N_RES
    return table.reshape(seq // BLK_ROWS, steps, N_RES, -1).transpose(0, 2, 1, 3).reshape(seq, -1)


def _perm_matrix(n_rows):
    steps = n_rows // N_RES
    idx = np.arange(n_rows)
    src = (idx % steps) * N_RES + idx // steps
    return jnp.asarray(src[:, None] == idx[None, :], dtype=BF16)


def kernel(x, norm_a, w_in_a, w_grp_a, scale_a, w_out_a, norm_kv, w_k, w_v, norm_b, w_in_b,
           w_out_b, norm_f):
    b, seq, d = x.shape
    n = b * seq
    n_a = w_in_a.shape[0]
    n_b = w_in_b.shape[0]
    e_b = w_k.shape[1]
    n_heads = e_b // HEAD_DIM
    assert seq % (N_RES * BAND) == 0 and d % HEAD_DIM == 0

    cos_nat, sin_nat = _rope_tables(seq)
    tables = jnp.stack(
        [cos_nat, sin_nat]
        + [_to_res_order(t, seq) * QK_SCALE for t in (cos_nat, sin_nat)]
        + [_to_blk_order(t, seq) * QK_SCALE for t in (cos_nat, sin_nat)])
    rope_nat, rope_res, rope_blk = (tables, 0, 1), (tables, 2, 3), (tables, 4, 5)
    pres, pblk = _perm_matrix(RES_ROWS), _perm_matrix(BLK_ROWS)

    xf = x.reshape(n, d)
    hdn = None

    w_out_b_bf = w_out_b.astype(BF16)
    for i in range(n_a):
        w_comb, w_gate, w_out_bf = _wcomb_call(w_in_a, w_grp_a, scale_a, w_out_a, layer=i,
                                               name=f"a{i}_wcomb")
        if hdn is None:
            h = _ah_call(xf, w_comb, w_gate, seq=seq, gain=norm_a[i], name=f"a{i}_h")
        else:
            h = _ah_call(hdn, w_comb, w_gate, seq=seq, name=f"a{i}_h")
        if i < n_a - 1:
            gains = norm_a[i + 1][None]
            specs = [("nat", BF16)]
        else:
            gains = jnp.stack([norm_kv, norm_b[0], norm_b[0]])
            specs = [("nat", BF16), ("blk", BF16), ("res", BF16)]
        outs = _bout_call(h, xf, w_out_bf, gains, pres, pblk, layer=None, seq=seq, emit_x=True,
                          norm_specs=specs, name=f"a{i}_out")
        xf = outs[0]
        if i < n_a - 1:
            hdn = outs[1]
        else:
            hdn_kv, hdn_blk, hdn_res = outs[1], outs[2], outs[3]

    k_nat, k_res = _proj_call(hdn_kv, w_k[None], seq=seq, col_off=0, n_cols=e_b,
                              tables=rope_nat, perm_mat=pres, name="k")
    v_nat, v_res = _proj_call(hdn_kv, w_v[None], seq=seq, col_off=0, n_cols=e_b,
                              perm_mat=pres, name="v")
    k_nat, v_nat = k_nat.reshape(b, seq, e_b), v_nat.reshape(b, seq, e_b)

    out = None
    res_shape = lambda c: (b, N_RES, seq // N_RES, c)
    for i in range(n_b):
        hdn_res2 = hdn_res.reshape(n, d)
        q0 = _proj_call(hdn_blk, w_in_b, layer=i, seq=seq, col_off=0, n_cols=e_b, tables=rope_blk,
                        name=f"b{i}_q0")
        q12 = _proj_call(hdn_res2, w_in_b, layer=i, seq=seq, col_off=e_b, n_cols=2 * e_b,
                         tables=rope_res, name=f"b{i}_q12")
        gate = _proj_call(hdn_res2, w_in_b, layer=i, seq=seq, col_off=3 * e_b, n_cols=e_b,
                          name=f"b{i}_gate")
        hg = _attn_call(q0.reshape(b, seq, e_b), k_nat, v_nat, q12.reshape(res_shape(2 * e_b)),
                        gate.reshape(res_shape(e_b)), k_res, v_res, n_heads=n_heads,
                        name=f"b{i}_attn")
        if i < n_b - 1:
            gains = jnp.stack([norm_b[i + 1], norm_b[i + 1]])
            xf, hdn_blk, hdn_res = _bout_call(
                hg, xf, w_out_b_bf, gains, pres, pblk, layer=i, seq=seq, emit_x=True,
                norm_specs=[("blk", BF16), ("res", BF16)], name=f"b{i}_out")
        else:
            (out,) = _bout_call(hg, xf, w_out_b_bf, norm_f[None], pres, pblk, layer=i, seq=seq,
                                emit_x=False, norm_specs=[("nat", F32)], name=f"b{i}_out")
    return out.reshape(b, seq, d)
```

```python
import functools
import math

import jax
import jax.numpy as jnp
import numpy as np
from jax import lax
from jax.experimental import pallas as pl
from jax.experimental.pallas import tpu as pltpu

F32 = jnp.float32
BF16 = jnp.bfloat16

RMS_EPS = 1e-6
POOL_WINDOWS = (2, 4, 8, 16)
POOL_HALO = 16
HEAD_DIM = 128
ROPE_THETA = 10000.0
NEG_INF = -1e30
N_RES = 16
DILATIONS = (1, 4, 16)
BAND = 128
SUBLANES = 8
LANES = 128
RES_ROWS = N_RES * N_RES
SUB_ROWS = RES_ROWS
BLK_ROWS = BAND
VMEM_LIMIT = 56 * 1024 * 1024
QK_SCALE = math.log2(math.e) / math.sqrt(HEAD_DIM)


def _cparams(n_axes):
    return pltpu.CompilerParams(
        dimension_semantics=("arbitrary",) * n_axes, vmem_limit_bytes=VMEM_LIMIT)


def _const_spec(shape, n_grid, layer=None):
    if layer is None:
        block, index = shape, (0,) * len(shape)
    else:
        block, index = (None,) + tuple(shape[1:]), (layer,) + (0,) * (len(shape) - 1)
    if n_grid == 1:
        return pl.BlockSpec(block, lambda i: index, pipeline_mode=pl.Buffered(1))
    return pl.BlockSpec(block, lambda j, i: index, pipeline_mode=pl.Buffered(1))


def _store_res(dst_ref, p_ref, rows, row0=0):
    steps = RES_ROWS // N_RES
    for sub in range(rows.shape[0] // RES_ROWS):
        pb = jnp.dot(p_ref[...], rows[sub * RES_ROWS:(sub + 1) * RES_ROWS, :],
                     preferred_element_type=F32).astype(dst_ref.dtype)
        s0 = row0 // N_RES + sub * steps
        for r in range(N_RES):
            dst_ref[r, s0:s0 + steps, :] = pb[r * steps:(r + 1) * steps, :]


def _emit_norms(xn, gains_ref, pres_ref, pblk_ref, out_refs, norm_specs, row0):
    inv = lax.rsqrt(jnp.mean(xn * xn, axis=-1, keepdims=True) + RMS_EPS)
    xh = xn * inv
    for k, (order, dtype) in enumerate(norm_specs):
        hd = (xh * gains_ref[k:k + 1, :]).astype(dtype)
        if order == "nat":
            out_refs[k][row0:row0 + xn.shape[0], :] = hd
        elif order == "res":
            _store_res(out_refs[k], pres_ref, hd, row0)
        else:
            for sub in range(xn.shape[0] // BLK_ROWS):
                src = slice(sub * BLK_ROWS, (sub + 1) * BLK_ROWS)
                dst = slice(row0 + sub * BLK_ROWS, row0 + (sub + 1) * BLK_ROWS)
                out_refs[k][dst, :] = jnp.dot(pblk_ref[...], hd[src, :],
                                              preferred_element_type=F32).astype(dtype)


def _norm_out(n, d, seq, bm, order, dtype):
    blocks_per_seq = seq // bm
    if order == "res":
        shape = jax.ShapeDtypeStruct((n // seq, N_RES, seq // N_RES, d), dtype)
        spec = pl.BlockSpec((None, N_RES, bm // N_RES, d),
                            lambda i: (i // blocks_per_seq, 0, i % blocks_per_seq, 0))
    else:
        shape = jax.ShapeDtypeStruct((n, d), dtype)
        spec = pl.BlockSpec((bm, d), lambda i: (i, 0))
    return shape, spec


def _proj_kernel(*refs, bm, bn, rope, perm_out):
    refs = list(refs)
    lhs_ref, w_ref = refs[:2]
    del refs[:2]
    if rope:
        cos_ref, sin_ref = refs[:2]
        del refs[:2]
    if perm_out:
        p_ref = refs.pop(0)
    o_ref = refs.pop(0)
    wbf_ref = refs.pop()

    @pl.when(pl.program_id(1) == 0)
    def _():
        wbf_ref[...] = w_ref[...].astype(BF16)

    sub_rows = bm if perm_out else SUB_ROWS
    for r0 in range(0, bm, sub_rows):
        rows = slice(r0, r0 + sub_rows)
        acc = jnp.dot(lhs_ref[rows, :], wbf_ref[...], preferred_element_type=F32)
        if rope:
            cos = cos_ref[rows, :]
            sin = sin_ref[rows, :]
            heads = []
            for hh in range(bn // HEAD_DIM):
                t = acc[:, hh * HEAD_DIM:(hh + 1) * HEAD_DIM]
                rot = pltpu.roll(t, HEAD_DIM // 2, 1)
                heads.append((t * cos + rot * sin).astype(o_ref.dtype))
            res = jnp.concatenate(heads, axis=1)
        else:
            res = acc.astype(o_ref.dtype)
        o_ref[rows, :] = res
        if perm_out:
            _store_res(refs[0], p_ref, res, r0)


def _proj_call(lhs, w, *, seq, col_off, n_cols, name, layer=0, tables=None, perm_mat=None,
               bm=1024, bn=1024):
    n, k = lhs.shape
    blocks_per_seq = seq // bm
    col_blk0 = col_off // bn
    rope, perm_out = tables is not None, perm_mat is not None
    args = [lhs, w]
    in_specs = [pl.BlockSpec((bm, k), lambda j, i: (i, 0)),
                pl.BlockSpec((None, k, bn), lambda j, i: (layer, 0, j + col_blk0))]
    if rope:
        stacked, i_cos, i_sin = tables
        args += [stacked, stacked]
        in_specs += [pl.BlockSpec((None, bm, HEAD_DIM), lambda j, i: (i_cos, i % blocks_per_seq, 0)),
                     pl.BlockSpec((None, bm, HEAD_DIM), lambda j, i: (i_sin, i % blocks_per_seq, 0))]
    if perm_out:
        args.append(perm_mat)
        in_specs.append(_const_spec(perm_mat.shape, 2))
    out_shape = [jax.ShapeDtypeStruct((n, n_cols), BF16)]
    out_specs = [pl.BlockSpec((bm, bn), lambda j, i: (i, j))]
    if perm_out:
        out_shape.append(jax.ShapeDtypeStruct((n // seq, N_RES, seq // N_RES, n_cols), BF16))
        out_specs.append(pl.BlockSpec(
            (None, N_RES, bm // N_RES, bn),
            lambda j, i: (i // blocks_per_seq, 0, i % blocks_per_seq, j)))
    res = pl.pallas_call(
        functools.partial(_proj_kernel, bm=bm, bn=bn, rope=rope, perm_out=perm_out),
        grid=(n_cols // bn, n // bm),
        in_specs=in_specs,
        out_specs=out_specs,
        out_shape=out_shape,
        scratch_shapes=[pltpu.VMEM((k, bn), BF16)],
        compiler_params=_cparams(2),
        name=name,
    )(*args)
    return res if perm_out else res[0]


def _wcomb_kernel(wu_ref, wg_ref, wo_ref, grp_ref, scale_ref, wc_ref, wgbf_ref, wobf_ref):
    grp = (grp_ref[...] * scale_ref[...]).astype(BF16)
    wc_ref[...] = jnp.dot(wu_ref[...].astype(BF16), grp, preferred_element_type=F32).astype(BF16)
    wgbf_ref[...] = wg_ref[...].astype(BF16)
    wobf_ref[...] = wo_ref[...].astype(BF16)


def _wcomb_call(w_in, w_grp, scale, w_out, *, layer, name, bk=1024):
    n_layers, k, e2 = w_in.shape
    n_grp, gc = w_grp.shape[1], w_grp.shape[2]
    e = e2 // 2
    assert w_out.shape[1:] == (e, k) and e == k
    w_shape = jax.ShapeDtypeStruct((k, e), BF16)
    return pl.pallas_call(
        _wcomb_kernel,
        grid=(n_grp, k // bk),
        in_specs=[pl.BlockSpec((None, bk, gc), lambda g, i: (layer, i, g)),
                  pl.BlockSpec((None, bk, gc), lambda g, i: (layer, i, n_grp + g)),
                  pl.BlockSpec((None, bk, gc), lambda g, i: (layer, i, g)),
                  pl.BlockSpec((None, None, gc, gc), lambda g, i: (layer, g, 0, 0)),
                  pl.BlockSpec((None, 1, gc), lambda g, i: (layer * n_grp + g, 0, 0))],
        out_specs=[pl.BlockSpec((bk, gc), lambda g, i: (i, g))] * 3,
        out_shape=[w_shape, w_shape, w_shape],
        compiler_params=_cparams(2),
        name=name,
    )(w_in, w_in, w_out, w_grp, scale.reshape(n_layers * n_grp, 1, gc))


def _ah_kernel(hdn_ref, wc_ref, wg_ref, *rest, bm, blocks_per_seq, normed):
    gain_ref = rest[0] if normed else None
    h_ref, carry_ref = rest[-2:]
    blk = pl.program_id(0) % blocks_per_seq
    gc = h_ref.shape[1] // len(POOL_WINDOWS)

    @pl.when(blk == 0)
    def _():
        carry_ref[...] = jnp.zeros_like(carry_ref)

    if gain_ref is None:
        hdn = hdn_ref[...]
    else:
        xf = hdn_ref[...]
        inv = lax.rsqrt(jnp.mean(xf * xf, axis=-1, keepdims=True) + RMS_EPS)
        hdn = ((xf * inv) * gain_ref[...]).astype(BF16)
    pos = blk * bm + lax.broadcasted_iota(jnp.int32, (bm, 1), 0)

    def finish(g, acc_y, acc_g):
        w = POOL_WINDOWS[g]
        cols = slice(g * gc, (g + 1) * gc)
        ext = jnp.concatenate([carry_ref[:, cols], acc_y], axis=0)
        carry_ref[:, cols] = acc_y[bm - POOL_HALO:, :]
        wsum, have = ext, 1
        while have < w:
            wsum = wsum + pltpu.roll(wsum, have, 0)
            have *= 2
        inv_cnt = 1.0 / jnp.minimum(pos + 1, w).astype(F32)
        pooled = wsum[POOL_HALO:, :] * inv_cnt - acc_y
        h_ref[:, cols] = (pooled * (acc_g / (1.0 + jnp.exp(-acc_g)))).astype(h_ref.dtype)

    for g in reversed(range(len(POOL_WINDOWS))):
        cols = slice(g * gc, (g + 1) * gc)
        finish(g, jnp.dot(hdn, wc_ref[:, cols], preferred_element_type=F32),
               jnp.dot(hdn, wg_ref[:, cols], preferred_element_type=F32))


def _ah_call(hdn, w_comb, w_gate, *, seq, name, gain=None, bm=1024):
    n, k = hdn.shape
    e = w_comb.shape[1]
    args = [hdn, w_comb, w_gate]
    in_specs = [pl.BlockSpec((bm, k), lambda i: (i, 0)),
                _const_spec(w_comb.shape, 1),
                _const_spec(w_gate.shape, 1)]
    if gain is not None:
        args.append(gain.reshape(1, k))
        in_specs.append(_const_spec((1, k), 1))
    kern = functools.partial(_ah_kernel, bm=bm, blocks_per_seq=seq // bm, normed=gain is not None)
    return pl.pallas_call(
        kern,
        grid=(n // bm,),
        in_specs=in_specs,
        out_specs=pl.BlockSpec((bm, e), lambda i: (i, 0)),
        out_shape=jax.ShapeDtypeStruct((n, e), BF16),
        scratch_shapes=[pltpu.VMEM((POOL_HALO, e), F32)],
        compiler_params=_cparams(1),
        name=name,
    )(*args)


def _bout_kernel(hg_ref, x_ref, wout_ref, gains_ref, pres_ref, pblk_ref, *rest,
                 bm, res_in, emit_x, norm_specs):
    outs = rest
    steps = RES_ROWS // N_RES
    n_sub = bm // RES_ROWS
    hs = []
    for sub in range(n_sub):
        if not res_in:
            hs.append(hg_ref[sub * RES_ROWS:(sub + 1) * RES_ROWS, :])
            continue
        blk = jnp.concatenate(
            [hg_ref[r, sub * steps:(sub + 1) * steps, :] for r in range(N_RES)], axis=0)
        hs.append(jnp.dot(pres_ref[...], blk, preferred_element_type=F32).astype(BF16))
    xns = []
    for sub in range(n_sub):
        rows = slice(sub * RES_ROWS, (sub + 1) * RES_ROWS)
        xns.append(x_ref[rows, :] + jnp.dot(hs[sub], wout_ref[...], preferred_element_type=F32))
    for sub in range(n_sub):
        rows = slice(sub * RES_ROWS, (sub + 1) * RES_ROWS)
        if emit_x:
            outs[0][rows, :] = xns[sub]
        _emit_norms(xns[sub], gains_ref, pres_ref, pblk_ref, outs[1:] if emit_x else outs,
                    norm_specs, sub * RES_ROWS)


def _bout_call(hg, x, w_out, gains, pres, pblk, *, layer, seq, emit_x, norm_specs, name, bm=512):
    n, d = x.shape
    e = hg.shape[-1]
    blocks_per_seq = seq // bm
    res_in = hg.ndim == 4
    norm_shapes, norm_out_specs = zip(*[_norm_out(n, d, seq, bm, o, dt) for o, dt in norm_specs])
    out_specs, out_shape = list(norm_out_specs), list(norm_shapes)
    if emit_x:
        out_specs.insert(0, pl.BlockSpec((bm, d), lambda i: (i, 0)))
        out_shape.insert(0, jax.ShapeDtypeStruct((n, d), F32))
    kern = functools.partial(_bout_kernel, bm=bm, res_in=res_in, emit_x=emit_x,
                             norm_specs=tuple(norm_specs))
    if res_in:
        hg_spec = pl.BlockSpec((None, N_RES, bm // N_RES, e),
                               lambda i: (i // blocks_per_seq, 0, i % blocks_per_seq, 0))
    else:
        hg_spec = pl.BlockSpec((bm, e), lambda i: (i, 0))
    return pl.pallas_call(
        kern,
        grid=(n // bm,),
        in_specs=[hg_spec,
                  pl.BlockSpec((bm, d), lambda i: (i, 0)),
                  _const_spec(w_out.shape, 1, layer),
                  _const_spec(gains.shape, 1),
                  _const_spec(pres.shape, 1),
                  _const_spec(pblk.shape, 1)],
        out_specs=out_specs,
        out_shape=out_shape,
        compiler_params=_cparams(1),
        name=name,
    )(hg, x, w_out, gains, pres, pblk)


def _band_blocks(blocks):
    ss, ms = _band_scores(blocks)
    return [(acc, m, l) for (acc, l), m in zip(_band_outputs(blocks, ss, ms), ms)]


def _band_scores(blocks):
    nt = (((1,), (1,)), ((), ()))
    ss = [lax.dot_general(q, k, nt, preferred_element_type=F32) for q, k, _, _ in blocks]
    ss = [jnp.where(blk[3], s, NEG_INF) for blk, s in zip(blocks, ss)]
    return ss, [jnp.max(s, axis=-1, keepdims=True) for s in ss]


def _band_outputs(blocks, ss, shifts):
    ps = [jnp.exp2(s - m).astype(BF16) for s, m in zip(ss, shifts)]
    pvs = [jnp.dot(p, jnp.concatenate([blk[2], jnp.ones_like(blk[2])], axis=1),
                   preferred_element_type=F32) for blk, p in zip(blocks, ps)]
    return [(pv[:, :HEAD_DIM], pv[:, HEAD_DIM:]) for pv in pvs]


def _band_masks(row_idx, col_idx, row_idx1, col_idx1):
    band = (col_idx >= row_idx) & (col_idx <= row_idx + BAND)
    return band, col_idx1 <= row_idx1


def _attn_kernel(q0_ref, kn_ref, vn_ref, q1_ref, q2_ref, gate_ref, kp_ref, vp_ref, out_ref,
                 a0_s, m0_s, l0_s):
    seq = q0_ref.shape[0]
    steps = seq // N_RES
    row = lax.broadcasted_iota(jnp.int32, (BAND, 2 * BAND), 0)
    col = lax.broadcasted_iota(jnp.int32, (BAND, 2 * BAND), 1)
    row1 = lax.broadcasted_iota(jnp.int32, (BAND, BAND), 0)
    col1 = lax.broadcasted_iota(jnp.int32, (BAND, BAND), 1)
    mask_band, mask_first = _band_masks(row, col, row1, col1)
    blk_pos = lambda p: N_RES * (p % SUBLANES) + p // SUBLANES
    mask0_band, mask0_first = _band_masks(blk_pos(row), col, blk_pos(row1), col1)
    d4 = DILATIONS[1]
    chunk = BAND // d4
    n_sub = N_RES // d4
    d4_pos = lambda p: d4 * (p % chunk) + p // chunk
    mask4_band, mask4_first = _band_masks(
        d4_pos(row), d4_pos(col % BAND) + BAND * (col // BAND), d4_pos(row1), d4_pos(col1))

    def bcast(m):
        return jnp.broadcast_to(m, (m.shape[0], LANES))

    unroll0 = 16
    span = unroll0 * BAND
    tile = (N_RES, SUBLANES, LANES)

    def g0_group(it, first):
        base = 0 if first else pl.multiple_of(it * span, span)
        q_all = q0_ref[pl.ds(base, span), :]
        if first:
            k_all, v_all = kn_ref[0:span, :], vn_ref[0:span, :]
        else:
            lo = pl.multiple_of(base - BAND, BAND)
            k_all, v_all = kn_ref[pl.ds(lo, span + BAND), :], vn_ref[pl.ds(lo, span + BAND), :]
        blocks = []
        for u in range(unroll0):
            q = q_all[u * BAND:(u + 1) * BAND, :]
            if first and u == 0:
                blocks.append((q, k_all[0:BAND, :], v_all[0:BAND, :], mask0_first))
            else:
                k0 = (u - 1) * BAND if first else u * BAND
                blocks.append((q, k_all[k0:k0 + 2 * BAND, :], v_all[k0:k0 + 2 * BAND, :], mask0_band))
        for u, (acc, m, l) in enumerate(_band_blocks(blocks)):
            a0_s[it * unroll0 + u] = acc.reshape(tile)
            m0_s[it * unroll0 + u] = bcast(m).reshape(tile)
            l0_s[it * unroll0 + u] = l.reshape(tile)

    g0_group(0, True)

    def g0_body(it, carry):
        g0_group(it, False)
        return carry

    lax.fori_loop(1, seq // span, g0_body, 0)

    n_blk2 = steps // BAND
    n_blk1 = steps // chunk
    per_half = BAND // chunk

    def class_blocks(r4):
        def rows_of(ref, a):
            return [ref[r4 + d4 * k, a * chunk:(a + 1) * chunk, :] for k in range(n_sub)]

        blocks = []
        for k in range(n_sub):
            r = r4 + d4 * k
            q_all, k_all, v_all = q2_ref[r], kp_ref[r], vp_ref[r]
            for nb in range(n_blk2):
                q = q_all[nb * BAND:(nb + 1) * BAND, :]
                if nb == 0:
                    blocks.append((q, k_all[0:BAND, :], v_all[0:BAND, :], mask_first))
                else:
                    k0 = (nb - 1) * BAND
                    blocks.append((q, k_all[k0:k0 + 2 * BAND, :], v_all[k0:k0 + 2 * BAND, :], mask_band))
        for a in range(n_blk1):
            q = jnp.concatenate(rows_of(q1_ref, a), axis=0)
            if a == 0:
                blocks.append((q, jnp.concatenate(rows_of(kp_ref, 0), axis=0),
                               jnp.concatenate(rows_of(vp_ref, 0), axis=0), mask4_first))
            else:
                blocks.append((q,
                               jnp.concatenate(rows_of(kp_ref, a - 1) + rows_of(kp_ref, a), axis=0),
                               jnp.concatenate(rows_of(vp_ref, a - 1) + rows_of(vp_ref, a), axis=0),
                               mask4_band))
        ss, ms = _band_scores(blocks)
        ms2, ms1 = ms[:n_sub * n_blk2], ms[n_sub * n_blk2:]

        m0s, tops = {}, {}
        for k in range(n_sub):
            r = r4 + d4 * k
            rows1 = slice(k * chunk, (k + 1) * chunk)
            for nb in range(n_blk2):
                blk0 = slice(nb * BAND // SUBLANES, (nb + 1) * BAND // SUBLANES)
                m0 = m0_s[blk0, r].reshape(BAND, LANES)
                m1 = jnp.concatenate(
                    [bcast(m[rows1, :]) for m in ms1[nb * per_half:(nb + 1) * per_half]], axis=0)
                m0s[k, nb] = m0
                tops[k, nb] = jnp.maximum(jnp.maximum(m0, m1), bcast(ms2[k * n_blk2 + nb]))
        shifts = [tops[k, nb] for k in range(n_sub) for nb in range(n_blk2)]
        for a in range(n_blk1):
            piece = slice((a % per_half) * chunk, (a % per_half + 1) * chunk)
            shifts.append(jnp.concatenate(
                [tops[k, a // per_half][piece, :] for k in range(n_sub)], axis=0))
        shifts = [sh if s.shape[1] == LANES else jnp.concatenate([sh, sh], axis=1)
                  for sh, s in zip(shifts, ss)]
        return _band_outputs(blocks, ss, shifts), m0s, tops

    def class_merge(r4, state):
        outs, m0s, tops = state
        outs2, outs1 = outs[:n_sub * n_blk2], outs[n_sub * n_blk2:]
        for k in range(n_sub):
            r = r4 + d4 * k
            rows1 = slice(k * chunk, (k + 1) * chunk)
            for nb in range(n_blk2):
                rows = slice(nb * BAND, (nb + 1) * BAND)
                blk0 = slice(nb * BAND // SUBLANES, (nb + 1) * BAND // SUBLANES)
                a0 = a0_s[blk0, r].reshape(BAND, LANES)
                l0 = l0_s[blk0, r].reshape(BAND, LANES)
                part1 = outs1[nb * per_half:(nb + 1) * per_half]
                a1 = jnp.concatenate([acc[rows1, :] for acc, _ in part1], axis=0)
                l1 = jnp.concatenate([l[rows1, :] for _, l in part1], axis=0)
                a2, l2 = outs2[k * n_blk2 + nb]
                c0 = jnp.exp2(m0s[k, nb] - tops[k, nb])
                num = c0 * a0 + a1 + a2
                den = c0 * l0 + l1 + l2
                gt = gate_ref[r, rows, :].astype(F32)
                out_ref[r, rows, :] = ((num * gt) / (den * (1.0 + jnp.exp(-gt)))).astype(out_ref.dtype)

    pending = class_blocks(0)
    for r4 in range(d4):
        upcoming = class_blocks(r4 + 1) if r4 + 1 < d4 else None
        class_merge(r4, pending)
        pending = upcoming


def _attn_call(q0, k_nat, v_nat, q12, gate, k_res, v_res, *, n_heads, name):
    b, seq, e = q0.shape
    steps = seq // N_RES
    nat = lambda off: pl.BlockSpec((None, seq, HEAD_DIM), lambda bi, h: (bi, 0, h + off))
    res = lambda off: pl.BlockSpec((None, N_RES, steps, HEAD_DIM), lambda bi, h: (bi, 0, 0, h + off))
    blk_scratch = pltpu.VMEM((seq // BLK_ROWS, N_RES, SUBLANES, LANES), F32)
    return pl.pallas_call(
        _attn_kernel,
        grid=(b, n_heads),
        in_specs=[nat(0), nat(0), nat(0), res(0), res(n_heads), res(0), res(0), res(0)],
        out_specs=res(0),
        out_shape=jax.ShapeDtypeStruct((b, N_RES, steps, e), BF16),
        scratch_shapes=[blk_scratch] * 3,
        compiler_params=_cparams(2),
        name=name,
    )(q0, k_nat, v_nat, q12, q12, gate, k_res, v_res)


def _rope_tables(seq):
    p = np.arange(seq)
    steps = seq // N_RES
    token = np.concatenate([
        p,
        N_RES * (p % steps) + p // steps,
        (p // BLK_ROWS) * BLK_ROWS + N_RES * (p % SUBLANES) + (p % BLK_ROWS) // SUBLANES])
    scale = np.concatenate([np.ones(seq), np.full(2 * seq, QK_SCALE)]).astype(np.float32)[:, None]
    inv_freq = 1.0 / (ROPE_THETA ** (jnp.arange(0, HEAD_DIM, 2, dtype=F32) / HEAD_DIM))
    ang = jnp.asarray(token, dtype=F32)[:, None] * inv_freq[None, :]
    cos, sin = jnp.cos(ang) * scale, jnp.sin(ang) * scale
    cos = jnp.concatenate([cos, cos], axis=-1).reshape(3, seq, HEAD_DIM)
    sin = jnp.concatenate([-sin, sin], axis=-1).reshape(3, seq, HEAD_DIM)
    return jnp.stack([cos, sin], axis=1).reshape(6, seq, HEAD_DIM)


def _perm_matrix(n_rows):
    steps = n_rows // N_RES
    idx = np.arange(n_rows)
    src = (idx % steps) * N_RES + idx // steps
    return jnp.asarray(src[:, None] == idx[None, :], dtype=BF16)


def kernel(x, norm_a, w_in_a, w_grp_a, scale_a, w_out_a, norm_kv, w_k, w_v, norm_b, w_in_b,
           w_out_b, norm_f):
    b, seq, d = x.shape
    n = b * seq
    n_a = w_in_a.shape[0]
    n_b = w_in_b.shape[0]
    e_b = w_k.shape[1]
    n_heads = e_b // HEAD_DIM
    assert seq % (N_RES * BAND) == 0 and d % HEAD_DIM == 0

    tables = _rope_tables(seq)
    rope_nat, rope_res, rope_blk = (tables, 0, 1), (tables, 2, 3), (tables, 4, 5)
    pres, pblk = _perm_matrix(RES_ROWS), _perm_matrix(BLK_ROWS)

    xf = x.reshape(n, d)
    hdn = None

    w_out_b_bf = w_out_b.astype(BF16)
    for i in range(n_a):
        w_comb, w_gate, w_out_bf = _wcomb_call(w_in_a, w_grp_a, scale_a, w_out_a, layer=i,
                                               name=f"a{i}_wcomb")
        if hdn is None:
            h = _ah_call(xf, w_comb, w_gate, seq=seq, gain=norm_a[i], name=f"a{i}_h")
        else:
            h = _ah_call(hdn, w_comb, w_gate, seq=seq, name=f"a{i}_h")
        if i < n_a - 1:
            gains = norm_a[i + 1][None]
            specs = [("nat", BF16)]
        else:
            gains = jnp.stack([norm_kv, norm_b[0], norm_b[0]])
            specs = [("nat", BF16), ("blk", BF16), ("res", BF16)]
        outs = _bout_call(h, xf, w_out_bf, gains, pres, pblk, layer=None, seq=seq, emit_x=True,
                          norm_specs=specs, name=f"a{i}_out")
        xf = outs[0]
        if i < n_a - 1:
            hdn = outs[1]
        else:
            hdn_kv, hdn_blk, hdn_res = outs[1], outs[2], outs[3]

    k_nat, k_res = _proj_call(hdn_kv, w_k[None], seq=seq, col_off=0, n_cols=e_b,
                              tables=rope_nat, perm_mat=pres, name="k")
    v_nat, v_res = _proj_call(hdn_kv, w_v[None], seq=seq, col_off=0, n_cols=e_b,
                              perm_mat=pres, name="v")
    k_nat, v_nat = k_nat.reshape(b, seq, e_b), v_nat.reshape(b, seq, e_b)

    out = None
    res_shape = lambda c: (b, N_RES, seq // N_RES, c)
    for i in range(n_b):
        hdn_res2 = hdn_res.reshape(n, d)
        q0 = _proj_call(hdn_blk, w_in_b, layer=i, seq=seq, col_off=0, n_cols=e_b, tables=rope_blk,
                        name=f"b{i}_q0")
        q12 = _proj_call(hdn_res2, w_in_b, layer=i, seq=seq, col_off=e_b, n_cols=2 * e_b,
                         tables=rope_res, name=f"b{i}_q12")
        gate = _proj_call(hdn_res2, w_in_b, layer=i, seq=seq, col_off=3 * e_b, n_cols=e_b,
                          name=f"b{i}_gate")
        hg = _attn_call(q0.reshape(b, seq, e_b), k_nat, v_nat, q12.reshape(res_shape(2 * e_b)),
                        gate.reshape(res_shape(e_b)), k_res, v_res, n_heads=n_heads,
                        name=f"b{i}_attn")
        if i < n_b - 1:
            gains = jnp.stack([norm_b[i + 1], norm_b[i + 1]])
            xf, hdn_blk, hdn_res = _bout_call(
                hg, xf, w_out_b_bf, gains, pres, pblk, layer=i, seq=seq, emit_x=True,
                norm_specs=[("blk", BF16), ("res", BF16)], name=f"b{i}_out")
        else:
            (out,) = _bout_call(hg, xf, w_out_b_bf, norm_f[None], pres, pblk, layer=i, seq=seq,
                                emit_x=False, norm_specs=[("nat", F32)], name=f"b{i}_out")
    return out.reshape(b, seq, d)
```

```python
import functools
import math

import jax
import jax.numpy as jnp
import numpy as np
from jax import lax
from jax.experimental import pallas as pl
from jax.experimental.pallas import tpu as pltpu

F32 = jnp.float32
BF16 = jnp.bfloat16

RMS_EPS = 1e-6
POOL_WINDOWS = (2, 4, 8, 16)
POOL_HALO = 16
HEAD_DIM = 128
ROPE_THETA = 10000.0
NEG_INF = -1e30
N_RES = 16
DILATIONS = (1, 4, 16)
BAND = 128
SUBLANES = 8
LANES = 128
RES_ROWS = N_RES * N_RES
SUB_ROWS = RES_ROWS
BLK_ROWS = BAND
VMEM_LIMIT = 56 * 1024 * 1024
QK_SCALE = math.log2(math.e) / math.sqrt(HEAD_DIM)


def _cparams(n_axes):
    return pltpu.CompilerParams(
        dimension_semantics=("arbitrary",) * n_axes, vmem_limit_bytes=VMEM_LIMIT)


def _const_spec(shape, n_grid, layer=None):
    if layer is None:
        block, index = shape, (0,) * len(shape)
    else:
        block, index = (None,) + tuple(shape[1:]), (layer,) + (0,) * (len(shape) - 1)
    if n_grid == 1:
        return pl.BlockSpec(block, lambda i: index, pipeline_mode=pl.Buffered(1))
    return pl.BlockSpec(block, lambda j, i: index, pipeline_mode=pl.Buffered(1))


def _store_res(dst_ref, p_ref, rows, row0=0):
    steps = RES_ROWS // N_RES
    for sub in range(rows.shape[0] // RES_ROWS):
        pb = jnp.dot(p_ref[...], rows[sub * RES_ROWS:(sub + 1) * RES_ROWS, :],
                     preferred_element_type=F32).astype(dst_ref.dtype)
        s0 = row0 // N_RES + sub * steps
        for r in range(N_RES):
            dst_ref[r, s0:s0 + steps, :] = pb[r * steps:(r + 1) * steps, :]


def _emit_norms(xn, gains_ref, pres_ref, pblk_ref, out_refs, norm_specs, row0):
    inv = lax.rsqrt(jnp.mean(xn * xn, axis=-1, keepdims=True) + RMS_EPS)
    xh = xn * inv
    for k, (order, dtype) in enumerate(norm_specs):
        hd = (xh * gains_ref[k:k + 1, :]).astype(dtype)
        if order == "nat":
            out_refs[k][row0:row0 + xn.shape[0], :] = hd
        elif order == "res":
            _store_res(out_refs[k], pres_ref, hd, row0)
        else:
            for sub in range(xn.shape[0] // BLK_ROWS):
                src = slice(sub * BLK_ROWS, (sub + 1) * BLK_ROWS)
                dst = slice(row0 + sub * BLK_ROWS, row0 + (sub + 1) * BLK_ROWS)
                out_refs[k][dst, :] = jnp.dot(pblk_ref[...], hd[src, :],
                                              preferred_element_type=F32).astype(dtype)


def _norm_out(n, d, seq, bm, order, dtype):
    blocks_per_seq = seq // bm
    if order == "res":
        shape = jax.ShapeDtypeStruct((n // seq, N_RES, seq // N_RES, d), dtype)
        spec = pl.BlockSpec((None, N_RES, bm // N_RES, d),
                            lambda i: (i // blocks_per_seq, 0, i % blocks_per_seq, 0))
    else:
        shape = jax.ShapeDtypeStruct((n, d), dtype)
        spec = pl.BlockSpec((bm, d), lambda i: (i, 0))
    return shape, spec


def _proj_kernel(*refs, bm, bn, rope, perm_out):
    refs = list(refs)
    lhs_ref, w_ref = refs[:2]
    del refs[:2]
    if rope:
        cos_ref, sin_ref = refs[:2]
        del refs[:2]
    if perm_out:
        p_ref = refs.pop(0)
    o_ref = refs.pop(0)
    wbf_ref = refs.pop()

    @pl.when(pl.program_id(1) == 0)
    def _():
        wbf_ref[...] = w_ref[...].astype(BF16)

    sub_rows = bm if perm_out else (BLK_ROWS if rope else SUB_ROWS)
    for r0 in range(0, bm, sub_rows):
        rows = slice(r0, r0 + sub_rows)
        acc = jnp.dot(lhs_ref[rows, :], wbf_ref[...], preferred_element_type=F32)
        if rope:
            cos = cos_ref[rows, :]
            sin = sin_ref[rows, :]
            heads = []
            for hh in range(bn // HEAD_DIM):
                t = acc[:, hh * HEAD_DIM:(hh + 1) * HEAD_DIM]
                rot = pltpu.roll(t, HEAD_DIM // 2, 1)
                heads.append((t * cos + rot * sin).astype(o_ref.dtype))
            res = jnp.concatenate(heads, axis=1)
        else:
            res = acc.astype(o_ref.dtype)
        o_ref[rows, :] = res
        if perm_out:
            _store_res(refs[0], p_ref, res, r0)


def _proj_call(lhs, w, *, seq, col_off, n_cols, name, layer=0, tables=None, perm_mat=None,
               bm=1024, bn=1024):
    n, k = lhs.shape
    blocks_per_seq = seq // bm
    col_blk0 = col_off // bn
    rope, perm_out = tables is not None, perm_mat is not None
    args = [lhs, w]
    in_specs = [pl.BlockSpec((bm, k), lambda j, i: (i, 0)),
                pl.BlockSpec((None, k, bn), lambda j, i: (layer, 0, j + col_blk0))]
    if rope:
        stacked, i_cos, i_sin = tables
        args += [stacked, stacked]
        in_specs += [pl.BlockSpec((None, bm, HEAD_DIM), lambda j, i: (i_cos, i % blocks_per_seq, 0)),
                     pl.BlockSpec((None, bm, HEAD_DIM), lambda j, i: (i_sin, i % blocks_per_seq, 0))]
    if perm_out:
        args.append(perm_mat)
        in_specs.append(_const_spec(perm_mat.shape, 2))
    out_shape = [jax.ShapeDtypeStruct((n, n_cols), BF16)]
    out_specs = [pl.BlockSpec((bm, bn), lambda j, i: (i, j))]
    if perm_out:
        out_shape.append(jax.ShapeDtypeStruct((n // seq, N_RES, seq // N_RES, n_cols), BF16))
        out_specs.append(pl.BlockSpec(
            (None, N_RES, bm // N_RES, bn),
            lambda j, i: (i // blocks_per_seq, 0, i % blocks_per_seq, j)))
    res = pl.pallas_call(
        functools.partial(_proj_kernel, bm=bm, bn=bn, rope=rope, perm_out=perm_out),
        grid=(n_cols // bn, n // bm),
        in_specs=in_specs,
        out_specs=out_specs,
        out_shape=out_shape,
        scratch_shapes=[pltpu.VMEM((k, bn), BF16)],
        compiler_params=_cparams(2),
        name=name,
    )(*args)
    return res if perm_out else res[0]


def _wcomb_kernel(wu_ref, wg_ref, wo_ref, grp_ref, scale_ref, wc_ref, wgbf_ref, wobf_ref):
    grp = (grp_ref[...] * scale_ref[...]).astype(BF16)
    wc_ref[...] = jnp.dot(wu_ref[...].astype(BF16), grp, preferred_element_type=F32).astype(BF16)
    wgbf_ref[...] = wg_ref[...].astype(BF16)
    wobf_ref[...] = wo_ref[...].astype(BF16)


def _wcomb_call(w_in, w_grp, scale, w_out, *, layer, name, bk=1024):
    n_layers, k, e2 = w_in.shape
    n_grp, gc = w_grp.shape[1], w_grp.shape[2]
    e = e2 // 2
    assert w_out.shape[1:] == (e, k) and e == k
    w_shape = jax.ShapeDtypeStruct((k, e), BF16)
    return pl.pallas_call(
        _wcomb_kernel,
        grid=(n_grp, k // bk),
        in_specs=[pl.BlockSpec((None, bk, gc), lambda g, i: (layer, i, g)),
                  pl.BlockSpec((None, bk, gc), lambda g, i: (layer, i, n_grp + g)),
                  pl.BlockSpec((None, bk, gc), lambda g, i: (layer, i, g)),
                  pl.BlockSpec((None, None, gc, gc), lambda g, i: (layer, g, 0, 0)),
                  pl.BlockSpec((None, 1, gc), lambda g, i: (layer * n_grp + g, 0, 0))],
        out_specs=[pl.BlockSpec((bk, gc), lambda g, i: (i, g))] * 3,
        out_shape=[w_shape, w_shape, w_shape],
        compiler_params=_cparams(2),
        name=name,
    )(w_in, w_in, w_out, w_grp, scale.reshape(n_layers * n_grp, 1, gc))


def _ah_kernel(hdn_ref, wc_ref, wg_ref, *rest, bm, blocks_per_seq, normed):
    gain_ref = rest[0] if normed else None
    h_ref, carry_ref = rest[-2:]
    blk = pl.program_id(0) % blocks_per_seq
    gc = h_ref.shape[1] // len(POOL_WINDOWS)

    @pl.when(blk == 0)
    def _():
        carry_ref[...] = jnp.zeros_like(carry_ref)

    if gain_ref is None:
        hdn = hdn_ref[...]
    else:
        xf = hdn_ref[...]
        inv = lax.rsqrt(jnp.mean(xf * xf, axis=-1, keepdims=True) + RMS_EPS)
        hdn = ((xf * inv) * gain_ref[...]).astype(BF16)
    pos = blk * bm + lax.broadcasted_iota(jnp.int32, (bm, 1), 0)

    def finish(g, acc_y, acc_g):
        w = POOL_WINDOWS[g]
        cols = slice(g * gc, (g + 1) * gc)
        ext = jnp.concatenate([carry_ref[:, cols], acc_y], axis=0)
        carry_ref[:, cols] = acc_y[bm - POOL_HALO:, :]
        wsum, have = ext, 1
        while have < w:
            wsum = wsum + pltpu.roll(wsum, have, 0)
            have *= 2
        inv_cnt = 1.0 / jnp.minimum(pos + 1, w).astype(F32)
        pooled = wsum[POOL_HALO:, :] * inv_cnt - acc_y
        h_ref[:, cols] = (pooled * (acc_g / (1.0 + jnp.exp(-acc_g)))).astype(h_ref.dtype)

    for g in reversed(range(len(POOL_WINDOWS))):
        cols = slice(g * gc, (g + 1) * gc)
        finish(g, jnp.dot(hdn, wc_ref[:, cols], preferred_element_type=F32),
               jnp.dot(hdn, wg_ref[:, cols], preferred_element_type=F32))


def _ah_call(hdn, w_comb, w_gate, *, seq, name, gain=None, bm=1024):
    n, k = hdn.shape
    e = w_comb.shape[1]
    args = [hdn, w_comb, w_gate]
    in_specs = [pl.BlockSpec((bm, k), lambda i: (i, 0)),
                _const_spec(w_comb.shape, 1),
                _const_spec(w_gate.shape, 1)]
    if gain is not None:
        args.append(gain.reshape(1, k))
        in_specs.append(_const_spec((1, k), 1))
    kern = functools.partial(_ah_kernel, bm=bm, blocks_per_seq=seq // bm, normed=gain is not None)
    return pl.pallas_call(
        kern,
        grid=(n // bm,),
        in_specs=in_specs,
        out_specs=pl.BlockSpec((bm, e), lambda i: (i, 0)),
        out_shape=jax.ShapeDtypeStruct((n, e), BF16),
        scratch_shapes=[pltpu.VMEM((POOL_HALO, e), F32)],
        compiler_params=_cparams(1),
        name=name,
    )(*args)


def _bout_kernel(hg_ref, x_ref, wout_ref, gains_ref, pres_ref, pblk_ref, *rest,
                 bm, res_in, emit_x, norm_specs):
    outs = rest
    steps = RES_ROWS // N_RES
    n_sub = bm // RES_ROWS
    hs = []
    for sub in range(n_sub):
        if not res_in:
            hs.append(hg_ref[sub * RES_ROWS:(sub + 1) * RES_ROWS, :])
            continue
        blk = jnp.concatenate(
            [hg_ref[r, sub * steps:(sub + 1) * steps, :] for r in range(N_RES)], axis=0)
        hs.append(jnp.dot(pres_ref[...], blk, preferred_element_type=F32).astype(BF16))
    xns = []
    for sub in range(n_sub):
        rows = slice(sub * RES_ROWS, (sub + 1) * RES_ROWS)
        xns.append(x_ref[rows, :] + jnp.dot(hs[sub], wout_ref[...], preferred_element_type=F32))
    for sub in range(n_sub):
        rows = slice(sub * RES_ROWS, (sub + 1) * RES_ROWS)
        if emit_x:
            outs[0][rows, :] = xns[sub]
        _emit_norms(xns[sub], gains_ref, pres_ref, pblk_ref, outs[1:] if emit_x else outs,
                    norm_specs, sub * RES_ROWS)


def _bout_call(hg, x, w_out, gains, pres, pblk, *, layer, seq, emit_x, norm_specs, name, bm=512):
    n, d = x.shape
    e = hg.shape[-1]
    blocks_per_seq = seq // bm
    res_in = hg.ndim == 4
    norm_shapes, norm_out_specs = zip(*[_norm_out(n, d, seq, bm, o, dt) for o, dt in norm_specs])
    out_specs, out_shape = list(norm_out_specs), list(norm_shapes)
    if emit_x:
        out_specs.insert(0, pl.BlockSpec((bm, d), lambda i: (i, 0)))
        out_shape.insert(0, jax.ShapeDtypeStruct((n, d), F32))
    kern = functools.partial(_bout_kernel, bm=bm, res_in=res_in, emit_x=emit_x,
                             norm_specs=tuple(norm_specs))
    if res_in:
        hg_spec = pl.BlockSpec((None, N_RES, bm // N_RES, e),
                               lambda i: (i // blocks_per_seq, 0, i % blocks_per_seq, 0))
    else:
        hg_spec = pl.BlockSpec((bm, e), lambda i: (i, 0))
    return pl.pallas_call(
        kern,
        grid=(n // bm,),
        in_specs=[hg_spec,
                  pl.BlockSpec((bm, d), lambda i: (i, 0)),
                  _const_spec(w_out.shape, 1, layer),
                  _const_spec(gains.shape, 1),
                  _const_spec(pres.shape, 1),
                  _const_spec(pblk.shape, 1)],
        out_specs=out_specs,
        out_shape=out_shape,
        compiler_params=_cparams(1),
        name=name,
    )(hg, x, w_out, gains, pres, pblk)


def _band_blocks(blocks):
    ss, ms = _band_scores(blocks)
    return [(acc, m, l) for (acc, l), m in zip(_band_outputs(blocks, ss, ms), ms)]


def _band_scores(blocks):
    nt = (((1,), (1,)), ((), ()))
    ss = [lax.dot_general(q, k, nt, preferred_element_type=F32) for q, k, _, _ in blocks]
    ss = [jnp.where(blk[3], s, NEG_INF) for blk, s in zip(blocks, ss)]
    return ss, [jnp.max(s, axis=-1, keepdims=True) for s in ss]


def _band_outputs(blocks, ss, shifts):
    ps = [jnp.exp2(s - m).astype(BF16) for s, m in zip(ss, shifts)]
    pvs = [jnp.dot(p, jnp.concatenate([blk[2], jnp.ones_like(blk[2])], axis=1),
                   preferred_element_type=F32) for blk, p in zip(blocks, ps)]
    return [(pv[:, :HEAD_DIM], pv[:, HEAD_DIM:]) for pv in pvs]


def _band_masks(row_idx, col_idx, row_idx1, col_idx1):
    band = (col_idx >= row_idx) & (col_idx <= row_idx + BAND)
    return band, col_idx1 <= row_idx1


def _attn_kernel(q0_ref, kn_ref, vn_ref, q1_ref, q2_ref, gate_ref, kp_ref, vp_ref, out_ref,
                 a0_s, m0_s, l0_s):
    seq = q0_ref.shape[0]
    steps = seq // N_RES
    row = lax.broadcasted_iota(jnp.int32, (BAND, 2 * BAND), 0)
    col = lax.broadcasted_iota(jnp.int32, (BAND, 2 * BAND), 1)
    row1 = lax.broadcasted_iota(jnp.int32, (BAND, BAND), 0)
    col1 = lax.broadcasted_iota(jnp.int32, (BAND, BAND), 1)
    mask_band, mask_first = _band_masks(row, col, row1, col1)
    blk_pos = lambda p: N_RES * (p % SUBLANES) + p // SUBLANES
    mask0_band, mask0_first = _band_masks(blk_pos(row), col, blk_pos(row1), col1)
    d4 = DILATIONS[1]
    chunk = BAND // d4
    n_sub = N_RES // d4
    d4_pos = lambda p: d4 * (p % chunk) + p // chunk
    mask4_band, mask4_first = _band_masks(
        d4_pos(row), d4_pos(col % BAND) + BAND * (col // BAND), d4_pos(row1), d4_pos(col1))

    def bcast(m):
        return jnp.broadcast_to(m, (m.shape[0], LANES))

    unroll0 = 16
    span = unroll0 * BAND
    tile = (N_RES, SUBLANES, LANES)

    def g0_group(it, first):
        base = 0 if first else pl.multiple_of(it * span, span)
        q_all = q0_ref[pl.ds(base, span), :]
        if first:
            k_all, v_all = kn_ref[0:span, :], vn_ref[0:span, :]
        else:
            lo = pl.multiple_of(base - BAND, BAND)
            k_all, v_all = kn_ref[pl.ds(lo, span + BAND), :], vn_ref[pl.ds(lo, span + BAND), :]
        blocks = []
        for u in range(unroll0):
            q = q_all[u * BAND:(u + 1) * BAND, :]
            if first and u == 0:
                blocks.append((q, k_all[0:BAND, :], v_all[0:BAND, :], mask0_first))
            else:
                k0 = (u - 1) * BAND if first else u * BAND
                blocks.append((q, k_all[k0:k0 + 2 * BAND, :], v_all[k0:k0 + 2 * BAND, :], mask0_band))
        for u, (acc, m, l) in enumerate(_band_blocks(blocks)):
            a0_s[it * unroll0 + u] = acc.reshape(tile)
            m0_s[it * unroll0 + u] = bcast(m).reshape(tile)
            l0_s[it * unroll0 + u] = l.reshape(tile)

    g0_group(0, True)

    def g0_body(it, carry):
        g0_group(it, False)
        return carry

    lax.fori_loop(1, seq // span, g0_body, 0)

    n_blk2 = steps // BAND
    n_blk1 = steps // chunk
    per_half = BAND // chunk

    def class_blocks(r4):
        def rows_of(ref, a):
            return [ref[r4 + d4 * k, a * chunk:(a + 1) * chunk, :] for k in range(n_sub)]

        blocks = []
        for k in range(n_sub):
            r = r4 + d4 * k
            q_all, k_all, v_all = q2_ref[r], kp_ref[r], vp_ref[r]
            for nb in range(n_blk2):
                q = q_all[nb * BAND:(nb + 1) * BAND, :]
                if nb == 0:
                    blocks.append((q, k_all[0:BAND, :], v_all[0:BAND, :], mask_first))
                else:
                    k0 = (nb - 1) * BAND
                    blocks.append((q, k_all[k0:k0 + 2 * BAND, :], v_all[k0:k0 + 2 * BAND, :], mask_band))
        for a in range(n_blk1):
            q = jnp.concatenate(rows_of(q1_ref, a), axis=0)
            if a == 0:
                blocks.append((q, jnp.concatenate(rows_of(kp_ref, 0), axis=0),
                               jnp.concatenate(rows_of(vp_ref, 0), axis=0), mask4_first))
            else:
                blocks.append((q,
                               jnp.concatenate(rows_of(kp_ref, a - 1) + rows_of(kp_ref, a), axis=0),
                               jnp.concatenate(rows_of(vp_ref, a - 1) + rows_of(vp_ref, a), axis=0),
                               mask4_band))
        ss, ms = _band_scores(blocks)
        ms2, ms1 = ms[:n_sub * n_blk2], ms[n_sub * n_blk2:]

        m0s, tops = {}, {}
        for k in range(n_sub):
            r = r4 + d4 * k
            rows1 = slice(k * chunk, (k + 1) * chunk)
            for nb in range(n_blk2):
                blk0 = slice(nb * BAND // SUBLANES, (nb + 1) * BAND // SUBLANES)
                m0 = m0_s[blk0, r].reshape(BAND, LANES)
                m1 = jnp.concatenate(
                    [bcast(m[rows1, :]) for m in ms1[nb * per_half:(nb + 1) * per_half]], axis=0)
                m0s[k, nb] = m0
                tops[k, nb] = jnp.maximum(jnp.maximum(m0, m1), bcast(ms2[k * n_blk2 + nb]))
        shifts = [tops[k, nb] for k in range(n_sub) for nb in range(n_blk2)]
        for a in range(n_blk1):
            piece = slice((a % per_half) * chunk, (a % per_half + 1) * chunk)
            shifts.append(jnp.concatenate(
                [tops[k, a // per_half][piece, :] for k in range(n_sub)], axis=0))
        shifts = [sh if s.shape[1] == LANES else jnp.concatenate([sh, sh], axis=1)
                  for sh, s in zip(shifts, ss)]
        return _band_outputs(blocks, ss, shifts), m0s, tops

    def class_merge(r4, state):
        outs, m0s, tops = state
        outs2, outs1 = outs[:n_sub * n_blk2], outs[n_sub * n_blk2:]
        for k in range(n_sub):
            r = r4 + d4 * k
            rows1 = slice(k * chunk, (k + 1) * chunk)
            for nb in range(n_blk2):
                rows = slice(nb * BAND, (nb + 1) * BAND)
                blk0 = slice(nb * BAND // SUBLANES, (nb + 1) * BAND // SUBLANES)
                a0 = a0_s[blk0, r].reshape(BAND, LANES)
                l0 = l0_s[blk0, r].reshape(BAND, LANES)
                part1 = outs1[nb * per_half:(nb + 1) * per_half]
                a1 = jnp.concatenate([acc[rows1, :] for acc, _ in part1], axis=0)
                l1 = jnp.concatenate([l[rows1, :] for _, l in part1], axis=0)
                a2, l2 = outs2[k * n_blk2 + nb]
                c0 = jnp.exp2(m0s[k, nb] - tops[k, nb])
                num = c0 * a0 + a1 + a2
                den = c0 * l0 + l1 + l2
                gt = gate_ref[r, rows, :].astype(F32)
                out_ref[r, rows, :] = ((num * gt) / (den * (1.0 + jnp.exp(-gt)))).astype(out_ref.dtype)

    pending = class_blocks(0)
    for r4 in range(d4):
        upcoming = class_blocks(r4 + 1) if r4 + 1 < d4 else None
        class_merge(r4, pending)
        pending = upcoming


def _attn_call(q0, k_nat, v_nat, q12, gate, k_res, v_res, *, n_heads, name):
    b, seq, e = q0.shape
    steps = seq // N_RES
    nat = lambda off: pl.BlockSpec((None, seq, HEAD_DIM), lambda bi, h: (bi, 0, h + off))
    res = lambda off: pl.BlockSpec((None, N_RES, steps, HEAD_DIM), lambda bi, h: (bi, 0, 0, h + off))
    blk_scratch = pltpu.VMEM((seq // BLK_ROWS, N_RES, SUBLANES, LANES), F32)
    return pl.pallas_call(
        _attn_kernel,
        grid=(b, n_heads),
        in_specs=[nat(0), nat(0), nat(0), res(0), res(n_heads), res(0), res(0), res(0)],
        out_specs=res(0),
        out_shape=jax.ShapeDtypeStruct((b, N_RES, steps, e), BF16),
        scratch_shapes=[blk_scratch] * 3,
        compiler_params=_cparams(2),
        name=name,
    )(q0, k_nat, v_nat, q12, q12, gate, k_res, v_res)


def _rope_tables(seq):
    inv_freq = 1.0 / (ROPE_THETA ** (jnp.arange(0, HEAD_DIM, 2, dtype=F32) / HEAD_DIM))
    ang = jnp.arange(seq, dtype=F32)[:, None] * inv_freq[None, :]
    cos, sin = jnp.cos(ang), jnp.sin(ang)
    return jnp.concatenate([cos, cos], axis=-1), jnp.concatenate([-sin, sin], axis=-1)


def _to_res_order(table, seq):
    return table.reshape(seq // N_RES, N_RES, -1).transpose(1, 0, 2).reshape(seq, -1)


def _to_blk_order(table, seq):
    steps = BLK_ROWS // N_RES
    return table.reshape(seq // BLK_ROWS, steps, N_RES, -1).transpose(0, 2, 1, 3).reshape(seq, -1)


def _perm_matrix(n_rows):
    steps = n_rows // N_RES
    idx = np.arange(n_rows)
    src = (idx % steps) * N_RES + idx // steps
    return jnp.asarray(src[:, None] == idx[None, :], dtype=BF16)


def kernel(x, norm_a, w_in_a, w_grp_a, scale_a, w_out_a, norm_kv, w_k, w_v, norm_b, w_in_b,
           w_out_b, norm_f):
    b, seq, d = x.shape
    n = b * seq
    n_a = w_in_a.shape[0]
    n_b = w_in_b.shape[0]
    e_b = w_k.shape[1]
    n_heads = e_b // HEAD_DIM
    assert seq % (N_RES * BAND) == 0 and d % HEAD_DIM == 0

    cos_nat, sin_nat = _rope_tables(seq)
    tables = jnp.stack(
        [cos_nat, sin_nat]
        + [_to_res_order(t, seq) * QK_SCALE for t in (cos_nat, sin_nat)]
        + [_to_blk_order(t, seq) * QK_SCALE for t in (cos_nat, sin_nat)])
    rope_nat, rope_res, rope_blk = (tables, 0, 1), (tables, 2, 3), (tables, 4, 5)
    pres, pblk = _perm_matrix(RES_ROWS), _perm_matrix(BLK_ROWS)

    xf = x.reshape(n, d)
    hdn = None

    w_out_b_bf = w_out_b.astype(BF16)
    for i in range(n_a):
        w_comb, w_gate, w_out_bf = _wcomb_call(w_in_a, w_grp_a, scale_a, w_out_a, layer=i,
                                               name=f"a{i}_wcomb")
        if hdn is None:
            h = _ah_call(xf, w_comb, w_gate, seq=seq, gain=norm_a[i], name=f"a{i}_h")
        else:
            h = _ah_call(hdn, w_comb, w_gate, seq=seq, name=f"a{i}_h")
        if i < n_a - 1:
            gains = norm_a[i + 1][None]
            specs = [("nat", BF16)]
        else:
            gains = jnp.stack([norm_kv, norm_b[0], norm_b[0]])
            specs = [("nat", BF16), ("blk", BF16), ("res", BF16)]
        outs = _bout_call(h, xf, w_out_bf, gains, pres, pblk, layer=None, seq=seq, emit_x=True,
                          norm_specs=specs, name=f"a{i}_out")
        xf = outs[0]
        if i < n_a - 1:
            hdn = outs[1]
        else:
            hdn_kv, hdn_blk, hdn_res = outs[1], outs[2], outs[3]

    k_nat, k_res = _proj_call(hdn_kv, w_k[None], seq=seq, col_off=0, n_cols=e_b,
                              tables=rope_nat, perm_mat=pres, name="k")
    v_nat, v_res = _proj_call(hdn_kv, w_v[None], seq=seq, col_off=0, n_cols=e_b,
                              perm_mat=pres, name="v")
    k_nat, v_nat = k_nat.reshape(b, seq, e_b), v_nat.reshape(b, seq, e_b)

    out = None
    res_shape = lambda c: (b, N_RES, seq // N_RES, c)
    for i in range(n_b):
        hdn_res2 = hdn_res.reshape(n, d)
        q0 = _proj_call(hdn_blk, w_in_b, layer=i, seq=seq, col_off=0, n_cols=e_b, tables=rope_blk,
                        name=f"b{i}_q0")
        q12 = _proj_call(hdn_res2, w_in_b, layer=i, seq=seq, col_off=e_b, n_cols=2 * e_b,
                         tables=rope_res, name=f"b{i}_q12")
        gate = _proj_call(hdn_res2, w_in_b, layer=i, seq=seq, col_off=3 * e_b, n_cols=e_b,
                          name=f"b{i}_gate")
        hg = _attn_call(q0.reshape(b, seq, e_b), k_nat, v_nat, q12.reshape(res_shape(2 * e_b)),
                        gate.reshape(res_shape(e_b)), k_res, v_res, n_heads=n_heads,
                        name=f"b{i}_attn")
        if i < n_b - 1:
            gains = jnp.stack([norm_b[i + 1], norm_b[i + 1]])
            xf, hdn_blk, hdn_res = _bout_call(
                hg, xf, w_out_b_bf, gains, pres, pblk, layer=i, seq=seq, emit_x=True,
                norm_specs=[("blk", BF16), ("res", BF16)], name=f"b{i}_out")
        else:
            (out,) = _bout_call(hg, xf, w_out_b_bf, norm_f[None], pres, pblk, layer=i, seq=seq,
                                emit_x=False, norm_specs=[("nat", F32)], name=f"b{i}_out")
    return out.reshape(b, seq, d)
```
